```python
import math
import jax, jax.numpy as jnp
from jax import lax
import numpy as np

D_MODEL = 1024
BATCH = 32
SEQ = 2048
DEPTH = 1

FOX_HEADS = 8
FOX_HEAD_DIM = 128
FOX_WIDTH = FOX_HEADS * FOX_HEAD_DIM
FOX_BLOCK = 128
GDN_HEADS = 8
GDN_HEAD_K = 128
GDN_HEAD_V = 128
GDN_K_WIDTH = GDN_HEADS * GDN_HEAD_K
GDN_V_WIDTH = GDN_HEADS * GDN_HEAD_V
GDN_CONV = 4
GDN_CONV_CH = 2 * GDN_K_WIDTH + GDN_V_WIDTH
GDN_CHUNK = 64
D_FF = 2816
FFN_CONV = 3
EPS = 1e-6

IN_SIZES = (FOX_WIDTH, FOX_WIDTH, FOX_WIDTH, FOX_HEADS,
            GDN_K_WIDTH, GDN_K_WIDTH, GDN_V_WIDTH, GDN_HEADS, GDN_HEADS, GDN_V_WIDTH,
            D_MODEL, D_MODEL)
D_IN = sum(IN_SIZES)

kernel_name = "fox_gdn_parallel_hybrid_convffn"


def rmsnorm(x, g):
    xf = x.astype(jnp.float32)
    xf = xf * lax.rsqrt(jnp.mean(xf * xf, axis=-1, keepdims=True) + EPS)
    return (xf * g.astype(jnp.float32)).astype(x.dtype)


def l2norm(x):
    xf = x.astype(jnp.float32)
    return xf * lax.rsqrt(jnp.sum(xf * xf, axis=-1, keepdims=True) + EPS)


def causal_dwconv(x, w):
    K = w.shape[0]
    S = x.shape[1]
    xp = jnp.pad(x, ((0, 0), (K - 1, 0), (0, 0)))
    return sum(xp[:, i:i + S, :] * w[i] for i in range(K))


def split_in(h):
    idx = np.cumsum(np.array(IN_SIZES))[:-1].tolist()
    return jnp.split(h, idx, axis=-1)


def fox_attention(q, k, v, log_f):
    B, S, H, Dh = q.shape
    nb = S // FOX_BLOCK
    scale = Dh ** -0.5
    c = jnp.cumsum(log_f.astype(jnp.float32), axis=1).transpose(0, 2, 1)
    kh = k.transpose(0, 2, 1, 3)
    vh = v.transpose(0, 2, 1, 3)
    qb = q.reshape(B, nb, FOX_BLOCK, H, Dh).transpose(1, 0, 3, 2, 4)
    cb = c.reshape(B, H, nb, FOX_BLOCK).transpose(2, 0, 1, 3)
    key_pos = jnp.arange(S)

    def block(args):
        qi, ci, i = args
        s = jnp.einsum('bhqd,bhkd->bhqk', qi, kh).astype(jnp.float32) * scale
        s = s + (ci[..., :, None] - c[..., None, :])
        q_pos = i * FOX_BLOCK + jnp.arange(FOX_BLOCK)
        causal = key_pos[None, :] <= q_pos[:, None]
        p = jax.nn.softmax(jnp.where(causal, s, -jnp.inf), axis=-1)
        return jnp.einsum('bhqk,bhkd->bhqd', p.astype(vh.dtype), vh)

    o = lax.map(block, (qb, cb, jnp.arange(nb)))
    return o.transpose(1, 0, 3, 2, 4).reshape(B, S, H * Dh)


def gated_delta_rule(q, k, v, g, beta):
    B, S, H, dk = q.shape
    dv = v.shape[-1]
    C = GDN_CHUNK
    N = S // C
    f32 = jnp.float32

    def chunk4(t):
        return t.astype(f32).reshape(B, N, C, H, t.shape[-1]).transpose(0, 3, 1, 2, 4)

    def chunk3(t):
        return t.astype(f32).reshape(B, N, C, H).transpose(0, 3, 1, 2)

    qc = chunk4(q) * (dk ** -0.5)
    kc = chunk4(k)
    vc = chunk4(v)
    bc = chunk3(beta)
    gc = jnp.cumsum(chunk3(g), axis=-1)

    tri_incl = jnp.tril(jnp.ones((C, C), dtype=bool))
    tri_strict = jnp.tril(jnp.ones((C, C), dtype=bool), k=-1)
    diff = gc[..., :, None] - gc[..., None, :]
    decay = jnp.exp(jnp.where(tri_incl, diff, -jnp.inf))

    kb = kc * bc[..., None]
    A = jnp.where(tri_strict, jnp.einsum('bhncd,bhnmd->bhncm', kb, kc) * decay, 0.0)
    lhs = A + jnp.eye(C, dtype=f32)
    rhs = jnp.concatenate([vc * bc[..., None], kb * jnp.exp(gc)[..., None]], axis=-1)
    sol = lax.linalg.triangular_solve(lhs, rhs, left_side=True, lower=True, unit_diagonal=True)
    u_hat, w = sol[..., :dv], sol[..., dv:]

    attn = jnp.einsum('bhncd,bhnmd->bhncm', qc, kc) * decay
    q_dec = qc * jnp.exp(gc)[..., None]
    k_dec = kc * jnp.exp(gc[..., -1:] - gc)[..., None]
    g_last = jnp.exp(gc[..., -1])

    xs = tuple(jnp.moveaxis(t, 2, 0) for t in (u_hat, w, attn, q_dec, k_dec, g_last))

    def step(state, inp):
        u_hat_i, w_i, attn_i, q_dec_i, k_dec_i, gl_i = inp
        u = u_hat_i - jnp.einsum('bhcd,bhdv->bhcv', w_i, state)
        o = jnp.einsum('bhcd,bhdv->bhcv', q_dec_i, state) + jnp.einsum('bhcm,bhmv->bhcv', attn_i, u)
        state = state * gl_i[..., None, None] + jnp.einsum('bhcd,bhcv->bhdv', k_dec_i, u)
        return state, o

    s0 = jnp.zeros((B, H, dk, dv), f32)
    _, o = lax.scan(step, s0, xs)
    return o.transpose(1, 0, 3, 2, 4).reshape(B, S, H, dv).astype(v.dtype)


def setup_inputs(seed: int = 0) -> dict:
    key = jax.random.key(seed)
    ks = jax.random.split(key, 20)
    L, D = DEPTH, D_MODEL
    nrm = jax.random.normal
    x = nrm(ks[0], (BATCH, SEQ, D), jnp.float32)
    norm_mix = 1.0 + 0.05 * nrm(ks[1], (L, D), jnp.float32)
    w_in = nrm(ks[2], (L, D, D_IN), jnp.float32) * D ** -0.5
    fox_f_bias = 3.0 + 0.5 * nrm(ks[3], (L, FOX_HEADS), jnp.float32)
    gdn_conv_w = nrm(ks[4], (L, GDN_CONV, GDN_CONV_CH), jnp.float32) * GDN_CONV ** -0.5
    gdn_a_log = jnp.log(jax.random.uniform(ks[5], (L, GDN_HEADS), jnp.float32, 1.0, 16.0))
    dt = jnp.exp(jax.random.uniform(ks[6], (L, GDN_HEADS), jnp.float32, math.log(1e-3), math.log(1e-1)))
    gdn_dt_bias = dt + jnp.log(-jnp.expm1(-dt))
    gdn_norm = 1.0 + 0.05 * nrm(ks[7], (L, GDN_HEAD_V), jnp.float32)
    w_branch_fox = nrm(ks[8], (L, FOX_WIDTH, D), jnp.float32) * FOX_WIDTH ** -0.5
    w_branch_gdn = nrm(ks[9], (L, GDN_V_WIDTH, D), jnp.float32) * GDN_V_WIDTH ** -0.5
    w_out = nrm(ks[10], (L, D, D), jnp.float32) * D ** -0.5
    norm_ffn = 1.0 + 0.05 * nrm(ks[11], (L, D), jnp.float32)
    w_up = nrm(ks[12], (L, D, 2 * D_FF), jnp.float32) * D ** -0.5
    ffn_conv_w = nrm(ks[13], (L, FFN_CONV, 2 * D_FF), jnp.float32) * FFN_CONV ** -0.5
    w_down = nrm(ks[14], (L, D_FF, D), jnp.float32) * D_FF ** -0.5
    norm_final = 1.0 + 0.05 * nrm(ks[15], (D,), jnp.float32)
    return {"x": x, "norm_mix": norm_mix, "w_in": w_in, "fox_f_bias": fox_f_bias,
            "gdn_conv_w": gdn_conv_w, "gdn_a_log": gdn_a_log, "gdn_dt_bias": gdn_dt_bias,
            "gdn_norm": gdn_norm, "w_branch_fox": w_branch_fox, "w_branch_gdn": w_branch_gdn,
            "w_out": w_out, "norm_ffn": norm_ffn, "w_up": w_up, "ffn_conv_w": ffn_conv_w,
            "w_down": w_down, "norm_final": norm_final}


def reference(x, norm_mix, w_in, fox_f_bias, gdn_conv_w, gdn_a_log, gdn_dt_bias, gdn_norm,
              w_branch_fox, w_branch_gdn, w_out, norm_ffn, w_up, ffn_conv_w, w_down, norm_final):
    B, S, D = x.shape
    h = x
    for l in range(DEPTH):
        hn = rmsnorm(h, norm_mix[l])
        proj = jnp.einsum('bsd,de->bse', hn, w_in[l])
        (fq, fk, fv, ff, gq, gk, gv, ga, gb, gz, gate_fox, gate_gdn) = split_in(proj)

        log_f = jax.nn.log_sigmoid(ff.astype(jnp.float32) + fox_f_bias[l].astype(jnp.float32))
        y_fox = fox_attention(fq.reshape(B, S, FOX_HEADS, FOX_HEAD_DIM),
                              fk.reshape(B, S, FOX_HEADS, FOX_HEAD_DIM),
                              fv.reshape(B, S, FOX_HEADS, FOX_HEAD_DIM), log_f)

        qkv = jax.nn.silu(causal_dwconv(jnp.concatenate([gq, gk, gv], axis=-1), gdn_conv_w[l]))
        cq, ck, cv = jnp.split(qkv, [GDN_K_WIDTH, 2 * GDN_K_WIDTH], axis=-1)
        q = l2norm(cq.reshape(B, S, GDN_HEADS, GDN_HEAD_K))
        k = l2norm(ck.reshape(B, S, GDN_HEADS, GDN_HEAD_K))
        v = cv.reshape(B, S, GDN_HEADS, GDN_HEAD_V)
        g = -jnp.exp(gdn_a_log[l].astype(jnp.float32)) * jax.nn.softplus(
            ga.astype(jnp.float32) + gdn_dt_bias[l].astype(jnp.float32))
        beta = jax.nn.sigmoid(gb.astype(jnp.float32))
        o = gated_delta_rule(q, k, v, g, beta)
        o = rmsnorm(o, gdn_norm[l]) * jax.nn.silu(gz.reshape(B, S, GDN_HEADS, GDN_HEAD_V))
        y_gdn = o.reshape(B, S, GDN_V_WIDTH)

        y = (jax.nn.sigmoid(gate_fox) * jnp.einsum('bse,ed->bsd', y_fox, w_branch_fox[l])
             + jax.nn.sigmoid(gate_gdn) * jnp.einsum('bse,ed->bsd', y_gdn, w_branch_gdn[l]))
        h = h + jnp.einsum('bsd,de->bse', y, w_out[l])

        hn = rmsnorm(h, norm_ffn[l])
        up = causal_dwconv(jnp.einsum('bsd,df->bsf', hn, w_up[l]), ffn_conv_w[l])
        u_gate, u_val = jnp.split(up, 2, axis=-1)
        h = h + jnp.einsum('bsf,fd->bsd', jax.nn.silu(u_gate) * u_val, w_down[l])
    return rmsnorm(h, norm_final)
```

```python
import functools

import jax
import jax.numpy as jnp
from jax import lax
from jax.experimental import pallas as pl
from jax.experimental.pallas import tpu as pltpu

EPS = 1e-6
HEADS = 8
HEAD_DIM = 128
WIDTH = HEADS * HEAD_DIM
GDN_CONV = 4
GDN_CHUNK = 64
FFN_CONV = 3
LANES = 128
SUBLANES = 8
NEG_BIG = -1e30

COL_FQ, COL_FK, COL_FV, COL_GQ, COL_GK, COL_GV, COL_GZ, COL_GATE_FOX, COL_GATE_GDN = range(9)
N_COLBLOCKS = 9
LANE_FF, LANE_GA, LANE_GB = 0, HEADS, 2 * HEADS

VMEM_LIMIT = 56 * 1024 * 1024

_f32 = jnp.float32
_bf16 = jnp.bfloat16


def _dot(a, b):
    return jnp.dot(a, b, preferred_element_type=_f32)


def _dot_nt(a, b):
    return lax.dot_general(a, b, (((1,), (1,)), ((), ())), preferred_element_type=_f32)


def _dot_tn(a, b):
    return lax.dot_general(a, b, (((0,), (0,)), ((), ())), preferred_element_type=_f32)


def _sigmoid(z):
    return 1.0 / (1.0 + jnp.exp(-z))


def _silu(z):
    return z * _sigmoid(z)


def _params(sem, vmem=VMEM_LIMIT):
    return pltpu.CompilerParams(dimension_semantics=sem, vmem_limit_bytes=vmem)


def _in_proj_kernel(x_ref, g_ref, wb_ref, ws_ref, proj_ref, small_ref, hn_ref, *, n_sub):
    j = pl.program_id(1)

    @pl.when(j == 0)
    def _():
        x = x_ref[...]
        ms = jnp.mean(x * x, axis=-1, keepdims=True)
        hn = (x * lax.rsqrt(ms + EPS) * g_ref[...]).astype(_bf16)
        hn_ref[...] = hn
        small_ref[...] = _dot(hn, ws_ref[...])

    hn = hn_ref[...]
    tn = wb_ref.shape[1]
    sub = tn // n_sub
    for c in range(n_sub):
        proj_ref[:, c * sub:(c + 1) * sub] = _dot(hn, wb_ref[:, c * sub:(c + 1) * sub]).astype(_bf16)


def _in_proj(x2, g, w_big, w_small, *, tm, tn):
    T, D = x2.shape
    N = w_big.shape[1]
    return pl.pallas_call(
        functools.partial(_in_proj_kernel, n_sub=tn // 768),
        grid=(T // tm, N // tn),
        in_specs=[
            pl.BlockSpec((tm, D), lambda i, j: (i, 0)),
            pl.BlockSpec((1, D), lambda i, j: (0, 0)),
            pl.BlockSpec((D, tn), lambda i, j: (0, j)),
            pl.BlockSpec((D, LANES), lambda i, j: (0, 0)),
        ],
        out_specs=[
            pl.BlockSpec((tm, tn), lambda i, j: (i, j)),
            pl.BlockSpec((tm, LANES), lambda i, j: (i, 0)),
        ],
        out_shape=[
            jax.ShapeDtypeStruct((T, N), _bf16),
            jax.ShapeDtypeStruct((T, LANES), _f32),
        ],
        scratch_shapes=[pltpu.VMEM((tm, D), _bf16)],
        compiler_params=_params(("arbitrary", "arbitrary")),
        name="in_proj",
    )(x2, g, w_big, w_small)


def _split3(v):
    hi = v.astype(_bf16)
    r1 = v - hi.astype(_f32)
    mid = r1.astype(_bf16)
    lo = (r1 - mid.astype(_f32)).astype(_bf16)
    return hi, mid, lo


def _gate_prep_kernel(raw_ref, p_ref, g_ref, gt_ref):
    S = raw_ref.shape[0]
    nblk = S // LANES
    lane = lax.broadcasted_iota(jnp.int32, (LANES, LANES), 1)
    row = lax.broadcasted_iota(jnp.int32, (LANES, LANES), 0)
    tri = (row >= lane)
    tri_full = jnp.where(tri, 1.0, 0.0).astype(_bf16)
    same_chunk = (row >= GDN_CHUNK) == (lane >= GDN_CHUNK)
    tri_chunk = jnp.where(tri & same_chunk, 1.0, 0.0).astype(_bf16)
    bias = p_ref[0:1, :]
    neg_a = -jnp.exp(p_ref[1:2, :])
    is_f = lane < LANE_GA
    is_g = lane < LANE_GB
    carry = jnp.zeros((1, LANES), _f32)
    for r in range(nblk):
        z = raw_ref[r * LANES:(r + 1) * LANES, :] + bias
        t = jnp.log(1.0 + jnp.exp(-jnp.abs(z)))
        logsig = jnp.minimum(z, 0.0) - t
        softplus = jnp.maximum(z, 0.0) + t
        sig = _sigmoid(z)
        vals = jnp.where(is_f, logsig, jnp.where(is_g, neg_a * softplus, sig))
        hi, mid, lo = _split3(vals)
        cum_full = (_dot(tri_full, hi) + _dot(tri_full, mid)) + _dot(tri_full, lo) + carry
        cum_chunk = (_dot(tri_chunk, hi) + _dot(tri_chunk, mid)) + _dot(tri_chunk, lo)
        carry = cum_full[LANES - 1:LANES, :]
        out = jnp.where(is_f, cum_full, jnp.where(is_g, cum_chunk, vals))
        g_ref[r * LANES:(r + 1) * LANES, :] = out
        gt_ref[:, r * LANES:(r + 1) * LANES] = out.T


def _gate_prep(small, gate_params, B, S):
    T = small.shape[0]
    return pl.pallas_call(
        _gate_prep_kernel,
        grid=(B,),
        in_specs=[
            pl.BlockSpec((S, LANES), lambda b: (b, 0)),
            pl.BlockSpec((SUBLANES, LANES), lambda b: (0, 0)),
        ],
        out_specs=[
            pl.BlockSpec((S, LANES), lambda b: (b, 0)),
            pl.BlockSpec((None, LANES, S), lambda b: (b, 0, 0)),
        ],
        out_shape=[
            jax.ShapeDtypeStruct((T, LANES), _f32),
            jax.ShapeDtypeStruct((B, LANES, S), _f32),
        ],
        compiler_params=_params(("arbitrary",)),
        name="gate_prep",
    )(small, gate_params)


def _fox_kernel(q_ref, k_ref, v_ref, gt_ref, o_ref, negc_ref, m_ref, l_ref, acc_ref, *, tq):
    S = q_ref.shape[0]
    nq = S // tq
    h = pl.program_id(1)
    rows8 = lax.broadcasted_iota(jnp.int32, (SUBLANES, S), 0)
    crow = jnp.sum(jnp.where(rows8 == h, gt_ref[...], 0.0), axis=0, keepdims=True)
    for j in range(nq):
        negc_ref[j] = jnp.broadcast_to(-crow[:, j * tq:(j + 1) * tq], (SUBLANES, tq))
    ri = lax.broadcasted_iota(jnp.int32, (tq, tq), 0)
    ci = lax.broadcasted_iota(jnp.int32, (tq, tq), 1)
    causal = ci <= ri

    def tile(q, j, masked):
        start = pl.multiple_of(j * tq, tq)
        k = k_ref[pl.ds(start, tq), :]
        v = v_ref[pl.ds(start, tq), :]
        s = _dot_nt(q, k) + negc_ref[j][0:1, :]
        if masked:
            s = jnp.where(causal, s, NEG_BIG)
        m_prev = m_ref[...]
        m_new = jnp.maximum(m_prev, jnp.max(s, axis=-1, keepdims=True))
        alpha = jnp.exp(m_prev - m_new)
        p = jnp.exp(s - m_new)
        l_ref[...] = alpha * l_ref[...] + jnp.sum(p, axis=-1, keepdims=True)
        acc_ref[...] = alpha * acc_ref[...] + _dot(p.astype(_bf16), v)
        m_ref[...] = m_new

    def q_body(qi, carry):
        qstart = pl.multiple_of(qi * tq, tq)
        q = q_ref[pl.ds(qstart, tq), :]
        m_ref[...] = jnp.full(m_ref.shape, NEG_BIG, _f32)
        l_ref[...] = jnp.zeros(l_ref.shape, _f32)
        acc_ref[...] = jnp.zeros(acc_ref.shape, _f32)

        def k_body(j, c):
            tile(q, j, False)
            return c

        lax.fori_loop(0, qi, k_body, 0)
        tile(q, qi, True)
        o_ref[pl.ds(qstart, tq), :] = (acc_ref[...] / l_ref[...]).astype(o_ref.dtype)
        return carry

    lax.fori_loop(0, nq, q_body, 0)


def _fox(proj, gt, B, S, *, tq):
    T = proj.shape[0]
    nq = S // tq
    return pl.pallas_call(
        functools.partial(_fox_kernel, tq=tq),
        grid=(B, HEADS),
        in_specs=[
            pl.BlockSpec((S, HEAD_DIM), lambda b, h: (b, COL_FQ * HEADS + h)),
            pl.BlockSpec((S, HEAD_DIM), lambda b, h: (b, COL_FK * HEADS + h)),
            pl.BlockSpec((S, HEAD_DIM), lambda b, h: (b, COL_FV * HEADS + h)),
            pl.BlockSpec((None, SUBLANES, S), lambda b, h: (b, 0, 0)),
        ],
        out_specs=pl.BlockSpec((S, HEAD_DIM), lambda b, h: (b, h)),
        out_shape=jax.ShapeDtypeStruct((T, WIDTH), _bf16),
        scratch_shapes=[
            pltpu.VMEM((nq, SUBLANES, tq), _f32),
            pltpu.VMEM((tq, 1), _f32),
            pltpu.VMEM((tq, 1), _f32),
            pltpu.VMEM((tq, HEAD_DIM), _f32),
        ],
        compiler_params=_params(("arbitrary", "arbitrary")),
        name="fox",
    )(proj, proj, proj, gt)


def _gdn_kernel(q_ref, k_ref, v_ref, z_ref, g_ref, cw_ref, nw_ref, o_ref,
                xbuf_ref, cbuf_ref, state_ref, *, lc):
    C = GDN_CHUNK
    step = pl.program_id(1)

    @pl.when(step == 0)
    def _():
        xbuf_ref[0:SUBLANES, :] = jnp.zeros((SUBLANES, 3 * WIDTH), _f32)
        state_ref[...] = jnp.zeros(state_ref.shape, _f32)

    xbuf_ref[SUBLANES:, 0:WIDTH] = q_ref[...].astype(_f32)
    xbuf_ref[SUBLANES:, WIDTH:2 * WIDTH] = k_ref[...].astype(_f32)
    xbuf_ref[SUBLANES:, 2 * WIDTH:] = v_ref[...].astype(_f32)

    ri = lax.broadcasted_iota(jnp.int32, (C, C), 0)
    ci = lax.broadcasted_iota(jnp.int32, (C, C), 1)
    incl = ci <= ri
    strict = ci < ri
    scale = HEAD_DIM ** -0.5
    nw = nw_ref[...]

    def l2n(t):
        return t * lax.rsqrt(jnp.sum(t * t, axis=-1, keepdims=True) + EPS)

    for cb in range(3 * HEADS):
        col = cb * HEAD_DIM
        acc = None
        for i in range(GDN_CONV):
            off = SUBLANES - (GDN_CONV - 1) + i
            term = xbuf_ref[off:off + lc, col:col + HEAD_DIM] * cw_ref[i:i + 1, col:col + HEAD_DIM]
            acc = term if acc is None else acc + term
        act = _silu(acc)
        cbuf_ref[:, col:col + HEAD_DIM] = l2n(act) if cb < 2 * HEADS else act

    def head_chunk(r0, gblk, grows, h):
        col = h * HEAD_DIM
        q = cbuf_ref[pl.ds(r0, C), col:col + HEAD_DIM]
        k = cbuf_ref[pl.ds(r0, C), WIDTH + col:WIDTH + col + HEAD_DIM]
        v = cbuf_ref[pl.ds(r0, C), 2 * WIDTH + col:2 * WIDTH + col + HEAD_DIM]
        gcol = gblk[:, LANE_GA + h:LANE_GA + h + 1]
        bcol = gblk[:, LANE_GB + h:LANE_GB + h + 1]
        grow = grows[LANE_GA + h:LANE_GA + h + 1, :]
        glast = gcol[C - 1:C, :]
        decay = jnp.exp(jnp.where(incl, gcol - grow, NEG_BIG))
        e_col = jnp.exp(gcol)
        kb = k * bcol
        k16 = k.astype(_bf16)
        a = jnp.where(strict, _dot_nt(kb.astype(_bf16), k16) * decay, 0.0)
        attn = _dot_nt((q * scale).astype(_bf16), k16) * decay
        rhs = jnp.concatenate([v * bcol, kb * e_col], axis=-1).astype(_bf16)
        xp = -a
        r = xp
        for _ in range(5):
            x16 = xp.astype(_bf16)
            xp = _dot(x16, x16)
            r = r + xp + _dot(r.astype(_bf16), xp.astype(_bf16))
        sol = rhs.astype(_f32) + _dot(r.astype(_bf16), rhs)
        u_hat = sol[:, :HEAD_DIM]
        w = sol[:, HEAD_DIM:]
        st = state_ref[h]
        st16 = st.astype(_bf16)
        u = u_hat - _dot(w.astype(_bf16), st16)
        u16 = u.astype(_bf16)
        o = _dot((q * (scale * e_col)).astype(_bf16), st16) + _dot(attn.astype(_bf16), u16)
        k_dec = (k * jnp.exp(glast - gcol)).astype(_bf16)
        state_ref[h] = st * jnp.exp(glast) + _dot_tn(k_dec, u16)
        on = o * lax.rsqrt(jnp.mean(o * o, axis=-1, keepdims=True) + EPS) * nw
        zg = z_ref[pl.ds(r0, C), col:col + HEAD_DIM].astype(_f32)
        o_ref[pl.ds(r0, C), col:col + HEAD_DIM] = (on * _silu(zg)).astype(o_ref.dtype)

    def pair_body(n, carry):
        p0 = pl.multiple_of(n * LANES, LANES)
        gpair = g_ref[pl.ds(p0, LANES), :]
        gpair_t = gpair.T
        for c in range(LANES // C):
            r0 = p0 + c * C
            gblk = gpair[c * C:(c + 1) * C, :]
            grows = gpair_t[:, c * C:(c + 1) * C]
            for h in range(HEADS):
                head_chunk(r0, gblk, grows, h)
        return carry

    lax.fori_loop(0, lc // LANES, pair_body, 0)
    xbuf_ref[0:SUBLANES, :] = xbuf_ref[lc:lc + SUBLANES, :]


def _gdn(proj, g, conv_w, norm_w, B, S, *, lc):
    T = proj.shape[0]
    nsteps = S // lc
    row = lambda b, s: b * nsteps + s
    return pl.pallas_call(
        functools.partial(_gdn_kernel, lc=lc),
        grid=(B, nsteps),
        in_specs=[
            pl.BlockSpec((lc, WIDTH), lambda b, s: (row(b, s), COL_GQ)),
            pl.BlockSpec((lc, WIDTH), lambda b, s: (row(b, s), COL_GK)),
            pl.BlockSpec((lc, WIDTH), lambda b, s: (row(b, s), COL_GV)),
            pl.BlockSpec((lc, WIDTH), lambda b, s: (row(b, s), COL_GZ)),
            pl.BlockSpec((lc, LANES), lambda b, s: (row(b, s), 0)),
            pl.BlockSpec((GDN_CONV, 3 * WIDTH), lambda b, s: (0, 0)),
            pl.BlockSpec((1, HEAD_DIM), lambda b, s: (0, 0)),
        ],
        out_specs=pl.BlockSpec((lc, WIDTH), lambda b, s: (row(b, s), 0)),
        out_shape=jax.ShapeDtypeStruct((T, WIDTH), _bf16),
        scratch_shapes=[
            pltpu.VMEM((lc + SUBLANES, 3 * WIDTH), _f32),
            pltpu.VMEM((lc, 3 * WIDTH), _f32),
            pltpu.VMEM((HEADS, HEAD_DIM, HEAD_DIM), _f32),
        ],
        compiler_params=_params(("arbitrary", "arbitrary")),
        name="gdn",
    )(proj, proj, proj, proj, g, conv_w, norm_w)


def _merge_kernel(x_ref, yf_ref, yg_ref, gf_ref, gg_ref, wf_ref, wg_ref, wo_ref, o_ref):
    a = _sigmoid(gf_ref[...].astype(_f32)) * _dot(yf_ref[...], wf_ref[...])
    b = _sigmoid(gg_ref[...].astype(_f32)) * _dot(yg_ref[...], wg_ref[...])
    y = (a + b).astype(_bf16)
    o_ref[...] = x_ref[...] + _dot(y, wo_ref[...])


def _merge(x2, y_fox, y_gdn, proj, wf, wg, wo, *, tm):
    T, D = x2.shape
    rows = lambda i: (i, 0)
    const = lambda i: (0, 0)
    return pl.pallas_call(
        _merge_kernel,
        grid=(T // tm,),
        in_specs=[
            pl.BlockSpec((tm, D), rows),
            pl.BlockSpec((tm, WIDTH), rows),
            pl.BlockSpec((tm, WIDTH), rows),
            pl.BlockSpec((tm, D), lambda i: (i, COL_GATE_FOX)),
            pl.BlockSpec((tm, D), lambda i: (i, COL_GATE_GDN)),
            pl.BlockSpec((WIDTH, D), const),
            pl.BlockSpec((WIDTH, D), const),
            pl.BlockSpec((D, D), const),
        ],
        out_specs=pl.BlockSpec((tm, D), rows),
        out_shape=jax.ShapeDtypeStruct((T, D), _f32),
        compiler_params=_params(("arbitrary",)),
        name="merge",
    )(x2, y_fox, y_gdn, proj, proj, wf, wg, wo)


def _ffn_kernel(h_ref, g_ref, wu_ref, cw_ref, wd_ref, gfin_ref, o_ref,
                ubuf_ref, tail_ref, act_ref, *, tf, final_norm):
    tm = h_ref.shape[0]
    d_ff = wd_ref.shape[0]
    step = pl.program_id(1)

    @pl.when(step == 0)
    def _():
        tail_ref[...] = jnp.zeros(tail_ref.shape, _f32)

    x = h_ref[...]
    hn = (x * lax.rsqrt(jnp.mean(x * x, axis=-1, keepdims=True) + EPS) * g_ref[...]).astype(_bf16)

    def conv(c0, half):
        up = _dot(hn, wu_ref[:, c0:c0 + tf])
        lo = half * tf
        ubuf_ref[0:SUBLANES, lo:lo + tf] = tail_ref[:, c0:c0 + tf]
        ubuf_ref[SUBLANES:, lo:lo + tf] = up
        tail_ref[:, c0:c0 + tf] = up[tm - SUBLANES:, :]
        acc = None
        for i in range(FFN_CONV):
            off = SUBLANES - (FFN_CONV - 1) + i
            term = ubuf_ref[off:off + tm, lo:lo + tf] * cw_ref[i:i + 1, c0:c0 + tf]
            acc = term if acc is None else acc + term
        return acc

    for c in range(d_ff // tf):
        gate = conv(c * tf, 0)
        val = conv(d_ff + c * tf, 1)
        act_ref[:, c * tf:(c + 1) * tf] = (_silu(gate) * val).astype(_bf16)

    h2 = x + _dot(act_ref[...], wd_ref[...])
    if final_norm:
        h2 = h2 * lax.rsqrt(jnp.mean(h2 * h2, axis=-1, keepdims=True) + EPS) * gfin_ref[...]
    o_ref[...] = h2


def _ffn(h1, g, wu, conv_w, wd, gfin, B, S, *, tm, tf, final_norm):
    T, D = h1.shape
    d_ff = wd.shape[0]
    nsteps = S // tm
    rows = lambda b, s: (b * nsteps + s, 0)
    const = lambda b, s: (0, 0)
    single = pl.Buffered(1)
    return pl.pallas_call(
        functools.partial(_ffn_kernel, tf=tf, final_norm=final_norm),
        grid=(B, nsteps),
        in_specs=[
            pl.BlockSpec((tm, D), rows),
            pl.BlockSpec((1, D), const),
            pl.BlockSpec((D, 2 * d_ff), const, pipeline_mode=single),
            pl.BlockSpec((FFN_CONV, 2 * d_ff), const),
            pl.BlockSpec((d_ff, D), const, pipeline_mode=single),
            pl.BlockSpec((1, D), const),
        ],
        out_specs=pl.BlockSpec((tm, D), rows),
        out_shape=jax.ShapeDtypeStruct((T, D), _f32),
        scratch_shapes=[
            pltpu.VMEM((tm + SUBLANES, 2 * tf), _f32),
            pltpu.VMEM((SUBLANES, 2 * d_ff), _f32),
            pltpu.VMEM((tm, d_ff), _bf16),
        ],
        compiler_params=_params(("arbitrary", "arbitrary")),
        name="ffn",
    )(h1, g, wu, conv_w, wd, gfin)


def _tiles(S):
    pick = lambda pref: next(t for t in pref if S % t == 0)
    return dict(
        tm_in=pick((1024, 512, 256, 128)),
        tq=pick((256, 128)),
        lc=pick((256, 128, 64)),
        tm_merge=pick((512, 256, 128)),
        tm_ffn=pick((512, 256, 128)),
    )


def kernel(x, norm_mix, w_in, fox_f_bias, gdn_conv_w, gdn_a_log, gdn_dt_bias, gdn_norm,
           w_branch_fox, w_branch_gdn, w_out, norm_ffn, w_up, ffn_conv_w, w_down, norm_final):
    B, S, D = x.shape
    L = norm_mix.shape[0]
    T = B * S
    assert D == WIDTH and S % LANES == 0
    t = _tiles(S)
    d_ff = w_down.shape[1]
    tf = 256 if d_ff % 256 == 0 else LANES
    h = x.reshape(T, D)
    o = 0
    seg = {}
    for name, width in (("fq", WIDTH), ("fk", WIDTH), ("fv", WIDTH), ("ff", HEADS),
                        ("gq", WIDTH), ("gk", WIDTH), ("gv", WIDTH), ("ga", HEADS), ("gb", HEADS),
                        ("gz", WIDTH), ("gate_fox", D), ("gate_gdn", D)):
        seg[name] = (o, o + width)
        o += width
    for l in range(L):
        w = w_in[l]
        cols = lambda n: w[:, seg[n][0]:seg[n][1]]
        w_big = jnp.concatenate(
            [cols("fq") * (HEAD_DIM ** -0.5), cols("fk"), cols("fv"), cols("gq"), cols("gk"), cols("gv"),
             cols("gz"), cols("gate_fox"), cols("gate_gdn")], axis=1).astype(_bf16)
        w_small = jnp.concatenate(
            [cols("ff"), cols("ga"), cols("gb"), jnp.zeros((D, LANES - 3 * HEADS), w.dtype)], axis=1).astype(_bf16)
        zpad = jnp.zeros((LANES - 2 * HEADS,), _f32)
        gate_params = jnp.zeros((SUBLANES, LANES), _f32)
        gate_params = gate_params.at[0].set(jnp.concatenate([fox_f_bias[l], gdn_dt_bias[l], zpad]))
        gate_params = gate_params.at[1].set(jnp.concatenate([jnp.zeros((HEADS,), _f32), gdn_a_log[l], zpad]))

        proj, small = _in_proj(h, norm_mix[l][None, :], w_big, w_small, tm=t["tm_in"], tn=N_COLBLOCKS * WIDTH // 4)
        g, gt = _gate_prep(small, gate_params, B, S)
        y_fox = _fox(proj, gt, B, S, tq=t["tq"])
        y_gdn = _gdn(proj, g, gdn_conv_w[l], gdn_norm[l][None, :], B, S, lc=t["lc"])
        h1 = _merge(h, y_fox, y_gdn, proj, w_branch_fox[l].astype(_bf16), w_branch_gdn[l].astype(_bf16),
                    w_out[l].astype(_bf16), tm=t["tm_merge"])
        h = _ffn(h1, norm_ffn[l][None, :], w_up[l].astype(_bf16), ffn_conv_w[l], w_down[l].astype(_bf16),
                 norm_final[None, :], B, S, tm=t["tm_ffn"], tf=tf, final_norm=(l == L - 1))
    if L == 0:
        raise ValueError("at least one layer expected")
    return h.reshape(B, S, D)
```

```python
import functools

import jax
import jax.numpy as jnp
from jax import lax
from jax.experimental import pallas as pl
from jax.experimental.pallas import tpu as pltpu

EPS = 1e-6
HEADS = 8
HEAD_DIM = 128
WIDTH = HEADS * HEAD_DIM
GDN_CONV = 4
GDN_CHUNK = 64
FFN_CONV = 3
LANES = 128
SUBLANES = 8
NEG_BIG = -1e30

COL_FQ, COL_FK, COL_FV, COL_GQ, COL_GK, COL_GV, COL_GZ, COL_GATE_FOX, COL_GATE_GDN = range(9)
N_COLBLOCKS = 9
LANE_FF, LANE_GA, LANE_GB = 0, HEADS, 2 * HEADS

VMEM_LIMIT = 56 * 1024 * 1024

_f32 = jnp.float32
_bf16 = jnp.bfloat16


def _dot(a, b):
    return jnp.dot(a, b, preferred_element_type=_f32)


def _dot_nt(a, b):
    return lax.dot_general(a, b, (((1,), (1,)), ((), ())), preferred_element_type=_f32)


def _dot_tn(a, b):
    return lax.dot_general(a, b, (((0,), (0,)), ((), ())), preferred_element_type=_f32)


def _sigmoid(z):
    return 1.0 / (1.0 + jnp.exp(-z))


def _silu(z):
    return z * _sigmoid(z)


def _params(sem, vmem=VMEM_LIMIT):
    return pltpu.CompilerParams(dimension_semantics=sem, vmem_limit_bytes=vmem)


def _in_proj_kernel(x_ref, g_ref, wb_ref, ws_ref, proj_ref, small_ref, hn_ref, *, n_sub):
    j = pl.program_id(1)

    @pl.when(j == 0)
    def _():
        x = x_ref[...]
        ms = jnp.mean(x * x, axis=-1, keepdims=True)
        hn = (x * lax.rsqrt(ms + EPS) * g_ref[...]).astype(_bf16)
        hn_ref[...] = hn
        small_ref[...] = _dot(hn, ws_ref[...])

    hn = hn_ref[...]
    tn = wb_ref.shape[1]
    sub = tn // n_sub
    for c in range(n_sub):
        proj_ref[:, c * sub:(c + 1) * sub] = _dot(hn, wb_ref[:, c * sub:(c + 1) * sub]).astype(_bf16)


def _in_proj(x2, g, w_big, w_small, *, tm, tn):
    T, D = x2.shape
    N = w_big.shape[1]
    return pl.pallas_call(
        functools.partial(_in_proj_kernel, n_sub=tn // 768),
        grid=(T // tm, N // tn),
        in_specs=[
            pl.BlockSpec((tm, D), lambda i, j: (i, 0)),
            pl.BlockSpec((1, D), lambda i, j: (0, 0)),
            pl.BlockSpec((D, tn), lambda i, j: (0, j)),
            pl.BlockSpec((D, LANES), lambda i, j: (0, 0)),
        ],
        out_specs=[
            pl.BlockSpec((tm, tn), lambda i, j: (i, j)),
            pl.BlockSpec((tm, LANES), lambda i, j: (i, 0)),
        ],
        out_shape=[
            jax.ShapeDtypeStruct((T, N), _bf16),
            jax.ShapeDtypeStruct((T, LANES), _f32),
        ],
        scratch_shapes=[pltpu.VMEM((tm, D), _bf16)],
        compiler_params=_params(("arbitrary", "arbitrary")),
        name="in_proj",
    )(x2, g, w_big, w_small)


def _split3(v):
    hi = v.astype(_bf16)
    r1 = v - hi.astype(_f32)
    mid = r1.astype(_bf16)
    lo = (r1 - mid.astype(_f32)).astype(_bf16)
    return hi, mid, lo


def _gate_prep_kernel(raw_ref, p_ref, g_ref, nb_ref):
    S = raw_ref.shape[0]
    nblk = S // LANES
    lane = lax.broadcasted_iota(jnp.int32, (LANES, LANES), 1)
    row = lax.broadcasted_iota(jnp.int32, (LANES, LANES), 0)
    tri = (row >= lane)
    tri_full = jnp.where(tri, 1.0, 0.0).astype(_bf16)
    same_chunk = (row >= GDN_CHUNK) == (lane >= GDN_CHUNK)
    tri_chunk = jnp.where(tri & same_chunk, 1.0, 0.0).astype(_bf16)
    bias = p_ref[0:1, :]
    neg_a = -jnp.exp(p_ref[1:2, :])
    is_f = lane < LANE_GA
    is_g = lane < LANE_GB
    carry = jnp.zeros((1, LANES), _f32)
    for r in range(nblk):
        z = raw_ref[r * LANES:(r + 1) * LANES, :] + bias
        t = jnp.log(1.0 + jnp.exp(-jnp.abs(z)))
        logsig = jnp.minimum(z, 0.0) - t
        softplus = jnp.maximum(z, 0.0) + t
        sig = _sigmoid(z)
        vals = jnp.where(is_f, logsig, jnp.where(is_g, neg_a * softplus, sig))
        hi, mid, lo = _split3(vals)
        cum_full = (_dot(tri_full, hi) + _dot(tri_full, mid)) + _dot(tri_full, lo) + carry
        cum_chunk = (_dot(tri_chunk, hi) + _dot(tri_chunk, mid)) + _dot(tri_chunk, lo)
        carry = cum_full[LANES - 1:LANES, :]
        out = jnp.where(is_f, cum_full, jnp.where(is_g, cum_chunk, vals))
        g_ref[r * LANES:(r + 1) * LANES, :] = out
        for h in range(HEADS):
            nb_ref[h, r * LANES:(r + 1) * LANES, :] = jnp.broadcast_to(
                -out[:, LANE_FF + h:LANE_FF + h + 1], (LANES, LANES))


def _gate_prep(small, gate_params, B, S):
    T = small.shape[0]
    return pl.pallas_call(
        _gate_prep_kernel,
        grid=(B,),
        in_specs=[
            pl.BlockSpec((S, LANES), lambda b: (b, 0)),
            pl.BlockSpec((SUBLANES, LANES), lambda b: (0, 0)),
        ],
        out_specs=[
            pl.BlockSpec((S, LANES), lambda b: (b, 0)),
            pl.BlockSpec((None, HEADS, S, LANES), lambda b: (b, 0, 0, 0)),
        ],
        out_shape=[
            jax.ShapeDtypeStruct((T, LANES), _f32),
            jax.ShapeDtypeStruct((B, HEADS, S, LANES), _f32),
        ],
        compiler_params=_params(("arbitrary",)),
        name="gate_prep",
    )(small, gate_params)


def _fox_kernel(q_ref, k_ref, v_ref, nb_ref, o_ref, vt_ref, *, tq):
    S = q_ref.shape[0]
    nq = S // tq
    ri = lax.broadcasted_iota(jnp.int32, (tq, tq), 0)
    ci = lax.broadcasted_iota(jnp.int32, (tq, tq), 1)
    visible = ri <= ci
    eye = jnp.where(lax.broadcasted_iota(jnp.int32, (HEAD_DIM, HEAD_DIM), 0)
                    == lax.broadcasted_iota(jnp.int32, (HEAD_DIM, HEAD_DIM), 1), 1.0, 0.0).astype(_bf16)
    for j in range(nq):
        vt_ref[:, j * tq:(j + 1) * tq] = _dot_nt(eye, v_ref[j * tq:(j + 1) * tq, :]).astype(_bf16)

    def score(i, j):
        s = _dot_nt(k_ref[j * tq:(j + 1) * tq, :], q_ref[i * tq:(i + 1) * tq, :])
        nb = nb_ref[j * tq:(j + 1) * tq, :]
        s = s + jnp.concatenate([nb] * (tq // LANES), axis=-1)
        if i == j:
            s = jnp.where(visible, s, NEG_BIG)
        return s

    tiles = [(i, j) for i in range(nq) for j in range(i + 1)]
    s_next = score(*tiles[0])
    for t, (i, j) in enumerate(tiles):
        s = s_next
        if t + 1 < len(tiles):
            s_next = score(*tiles[t + 1])
        if j == 0:
            m = jnp.full((1, tq), NEG_BIG, _f32)
            l = jnp.zeros((1, tq), _f32)
            acc = jnp.zeros((HEAD_DIM, tq), _f32)
        m_new = jnp.maximum(m, jnp.max(s, axis=0, keepdims=True))
        alpha = jnp.exp(m - m_new)
        p = jnp.exp(s - m_new)
        l = alpha * l + jnp.sum(p, axis=0, keepdims=True)
        acc = alpha * acc + _dot(vt_ref[:, j * tq:(j + 1) * tq], p.astype(_bf16))
        m = m_new
        if j == i:
            o_ref[i * tq:(i + 1) * tq, :] = (acc * (1.0 / l)).T.astype(o_ref.dtype)


def _fox(proj, negc, B, S, *, tq):
    T = proj.shape[0]
    return pl.pallas_call(
        functools.partial(_fox_kernel, tq=tq),
        grid=(B, HEADS),
        in_specs=[
            pl.BlockSpec((S, HEAD_DIM), lambda b, h: (b, COL_FQ * HEADS + h)),
            pl.BlockSpec((S, HEAD_DIM), lambda b, h: (b, COL_FK * HEADS + h)),
            pl.BlockSpec((S, HEAD_DIM), lambda b, h: (b, COL_FV * HEADS + h)),
            pl.BlockSpec((None, None, S, LANES), lambda b, h: (b, h, 0, 0)),
        ],
        out_specs=pl.BlockSpec((S, HEAD_DIM), lambda b, h: (b, h)),
        out_shape=jax.ShapeDtypeStruct((T, WIDTH), _bf16),
        scratch_shapes=[pltpu.VMEM((HEAD_DIM, S), _bf16)],
        compiler_params=_params(("arbitrary", "arbitrary")),
        name="fox",
    )(proj, proj, proj, negc)


def _gdn_kernel(q_ref, k_ref, v_ref, z_ref, g_ref, cw_ref, nw_ref, o_ref,
                xbuf_ref, cbuf_ref, state_ref, *, lc):
    C = GDN_CHUNK
    step = pl.program_id(1)

    @pl.when(step == 0)
    def _():
        xbuf_ref[0:SUBLANES, :] = jnp.zeros((SUBLANES, 3 * WIDTH), _f32)
        state_ref[...] = jnp.zeros(state_ref.shape, _f32)

    xbuf_ref[SUBLANES:, 0:WIDTH] = q_ref[...].astype(_f32)
    xbuf_ref[SUBLANES:, WIDTH:2 * WIDTH] = k_ref[...].astype(_f32)
    xbuf_ref[SUBLANES:, 2 * WIDTH:] = v_ref[...].astype(_f32)

    ri = lax.broadcasted_iota(jnp.int32, (C, C), 0)
    ci = lax.broadcasted_iota(jnp.int32, (C, C), 1)
    incl = ci <= ri
    strict = ci < ri
    scale = HEAD_DIM ** -0.5
    nw = nw_ref[...]

    def l2n(t):
        return t * lax.rsqrt(jnp.sum(t * t, axis=-1, keepdims=True) + EPS)

    for cb in range(3 * HEADS):
        col = cb * HEAD_DIM
        acc = None
        for i in range(GDN_CONV):
            off = SUBLANES - (GDN_CONV - 1) + i
            term = xbuf_ref[off:off + lc, col:col + HEAD_DIM] * cw_ref[i:i + 1, col:col + HEAD_DIM]
            acc = term if acc is None else acc + term
        act = _silu(acc)
        cbuf_ref[:, col:col + HEAD_DIM] = l2n(act) if cb < 2 * HEADS else act

    def chunk(r0, gblk, grows):
        hs = range(HEADS)
        rows = pl.ds(r0, C)
        q = [cbuf_ref[rows, h * HEAD_DIM:(h + 1) * HEAD_DIM] for h in hs]
        k = [cbuf_ref[rows, WIDTH + h * HEAD_DIM:WIDTH + (h + 1) * HEAD_DIM] for h in hs]
        v = [cbuf_ref[rows, 2 * WIDTH + h * HEAD_DIM:2 * WIDTH + (h + 1) * HEAD_DIM] for h in hs]
        gcol = [gblk[:, LANE_GA + h:LANE_GA + h + 1] for h in hs]
        bcol = [gblk[:, LANE_GB + h:LANE_GB + h + 1] for h in hs]
        grow = [grows[LANE_GA + h:LANE_GA + h + 1, :] for h in hs]
        glast = [gcol[h][C - 1:C, :] for h in hs]
        decay = [jnp.exp(jnp.where(incl, gcol[h] - grow[h], NEG_BIG)) for h in hs]
        e_col = [jnp.exp(gcol[h]) for h in hs]
        kb = [k[h] * bcol[h] for h in hs]
        k16 = [k[h].astype(_bf16) for h in hs]
        a = [jnp.where(strict, _dot_nt(kb[h].astype(_bf16), k16[h]) * decay[h], 0.0) for h in hs]
        attn = [_dot_nt((q[h] * scale).astype(_bf16), k16[h]) * decay[h] for h in hs]
        rhs = [jnp.concatenate([v[h] * bcol[h], kb[h] * e_col[h]], axis=-1).astype(_bf16) for h in hs]
        xp = [-a[h] for h in hs]
        r = xp
        for _ in range(5):
            x16 = [xp[h].astype(_bf16) for h in hs]
            xp = [_dot(x16[h], x16[h]) for h in hs]
            r = [r[h] + xp[h] + _dot(r[h].astype(_bf16), xp[h].astype(_bf16)) for h in hs]
        sol = [rhs[h].astype(_f32) + _dot(r[h].astype(_bf16), rhs[h]) for h in hs]
        st = [state_ref[h] for h in hs]
        st16 = [st[h].astype(_bf16) for h in hs]
        u = [sol[h][:, :HEAD_DIM] - _dot(sol[h][:, HEAD_DIM:].astype(_bf16), st16[h]) for h in hs]
        u16 = [u[h].astype(_bf16) for h in hs]
        o = [_dot((q[h] * (scale * e_col[h])).astype(_bf16), st16[h]) + _dot(attn[h].astype(_bf16), u16[h])
             for h in hs]
        k_dec = [(k[h] * jnp.exp(glast[h] - gcol[h])).astype(_bf16) for h in hs]
        for h in hs:
            state_ref[h] = st[h] * jnp.exp(glast[h]) + _dot_tn(k_dec[h], u16[h])
        for h in hs:
            on = o[h] * lax.rsqrt(jnp.mean(o[h] * o[h], axis=-1, keepdims=True) + EPS) * nw
            zg = z_ref[rows, h * HEAD_DIM:(h + 1) * HEAD_DIM].astype(_f32)
            o_ref[rows, h * HEAD_DIM:(h + 1) * HEAD_DIM] = (on * _silu(zg)).astype(o_ref.dtype)

    def pair_body(n, carry):
        p0 = pl.multiple_of(n * LANES, LANES)
        gpair = g_ref[pl.ds(p0, LANES), :]
        gpair_t = gpair.T
        for c in range(LANES // C):
            chunk(p0 + c * C, gpair[c * C:(c + 1) * C, :], gpair_t[:, c * C:(c + 1) * C])
        return carry

    lax.fori_loop(0, lc // LANES, pair_body, 0)
    xbuf_ref[0:SUBLANES, :] = xbuf_ref[lc:lc + SUBLANES, :]


def _gdn(proj, g, conv_w, norm_w, B, S, *, lc):
    T = proj.shape[0]
    nsteps = S // lc
    row = lambda b, s: b * nsteps + s
    return pl.pallas_call(
        functools.partial(_gdn_kernel, lc=lc),
        grid=(B, nsteps),
        in_specs=[
            pl.BlockSpec((lc, WIDTH), lambda b, s: (row(b, s), COL_GQ)),
            pl.BlockSpec((lc, WIDTH), lambda b, s: (row(b, s), COL_GK)),
            pl.BlockSpec((lc, WIDTH), lambda b, s: (row(b, s), COL_GV)),
            pl.BlockSpec((lc, WIDTH), lambda b, s: (row(b, s), COL_GZ)),
            pl.BlockSpec((lc, LANES), lambda b, s: (row(b, s), 0)),
            pl.BlockSpec((GDN_CONV, 3 * WIDTH), lambda b, s: (0, 0)),
            pl.BlockSpec((1, HEAD_DIM), lambda b, s: (0, 0)),
        ],
        out_specs=pl.BlockSpec((lc, WIDTH), lambda b, s: (row(b, s), 0)),
        out_shape=jax.ShapeDtypeStruct((T, WIDTH), _bf16),
        scratch_shapes=[
            pltpu.VMEM((lc + SUBLANES, 3 * WIDTH), _f32),
            pltpu.VMEM((lc, 3 * WIDTH), _f32),
            pltpu.VMEM((HEADS, HEAD_DIM, HEAD_DIM), _f32),
        ],
        compiler_params=_params(("arbitrary", "arbitrary")),
        name="gdn",
    )(proj, proj, proj, proj, g, conv_w, norm_w)


def _merge_kernel(x_ref, yf_ref, yg_ref, gf_ref, gg_ref, wf_ref, wg_ref, wo_ref, o_ref):
    a = _sigmoid(gf_ref[...].astype(_f32)) * _dot(yf_ref[...], wf_ref[...])
    b = _sigmoid(gg_ref[...].astype(_f32)) * _dot(yg_ref[...], wg_ref[...])
    y = (a + b).astype(_bf16)
    o_ref[...] = x_ref[...] + _dot(y, wo_ref[...])


def _merge(x2, y_fox, y_gdn, proj, wf, wg, wo, *, tm):
    T, D = x2.shape
    rows = lambda i: (i, 0)
    const = lambda i: (0, 0)
    return pl.pallas_call(
        _merge_kernel,
        grid=(T // tm,),
        in_specs=[
            pl.BlockSpec((tm, D), rows),
            pl.BlockSpec((tm, WIDTH), rows),
            pl.BlockSpec((tm, WIDTH), rows),
            pl.BlockSpec((tm, D), lambda i: (i, COL_GATE_FOX)),
            pl.BlockSpec((tm, D), lambda i: (i, COL_GATE_GDN)),
            pl.BlockSpec((WIDTH, D), const),
            pl.BlockSpec((WIDTH, D), const),
            pl.BlockSpec((D, D), const),
        ],
        out_specs=pl.BlockSpec((tm, D), rows),
        out_shape=jax.ShapeDtypeStruct((T, D), _f32),
        compiler_params=_params(("arbitrary",)),
        name="merge",
    )(x2, y_fox, y_gdn, proj, proj, wf, wg, wo)


def _ffn_kernel(h_ref, g_ref, wu_ref, cw_ref, wd_ref, gfin_ref, o_ref,
                ubuf_ref, tail_ref, act_ref, *, tf, final_norm):
    tm = h_ref.shape[0]
    d_ff = wd_ref.shape[0]
    step = pl.program_id(1)

    @pl.when(step == 0)
    def _():
        tail_ref[...] = jnp.zeros(tail_ref.shape, _f32)

    x = h_ref[...]
    hn = (x * lax.rsqrt(jnp.mean(x * x, axis=-1, keepdims=True) + EPS) * g_ref[...]).astype(_bf16)

    def conv(c0, half):
        up = _dot(hn, wu_ref[:, c0:c0 + tf])
        lo = half * tf
        ubuf_ref[0:SUBLANES, lo:lo + tf] = tail_ref[:, c0:c0 + tf]
        ubuf_ref[SUBLANES:, lo:lo + tf] = up
        tail_ref[:, c0:c0 + tf] = up[tm - SUBLANES:, :]
        acc = None
        for i in range(FFN_CONV):
            off = SUBLANES - (FFN_CONV - 1) + i
            term = ubuf_ref[off:off + tm, lo:lo + tf] * cw_ref[i:i + 1, c0:c0 + tf]
            acc = term if acc is None else acc + term
        return acc

    for c in range(d_ff // tf):
        gate = conv(c * tf, 0)
        val = conv(d_ff + c * tf, 1)
        act_ref[:, c * tf:(c + 1) * tf] = (_silu(gate) * val).astype(_bf16)

    h2 = x + _dot(act_ref[...], wd_ref[...])
    if final_norm:
        h2 = h2 * lax.rsqrt(jnp.mean(h2 * h2, axis=-1, keepdims=True) + EPS) * gfin_ref[...]
    o_ref[...] = h2


def _ffn(h1, g, wu, conv_w, wd, gfin, B, S, *, tm, tf, final_norm):
    T, D = h1.shape
    d_ff = wd.shape[0]
    nsteps = S // tm
    rows = lambda b, s: (b * nsteps + s, 0)
    const = lambda b, s: (0, 0)
    single = pl.Buffered(1)
    return pl.pallas_call(
        functools.partial(_ffn_kernel, tf=tf, final_norm=final_norm),
        grid=(B, nsteps),
        in_specs=[
            pl.BlockSpec((tm, D), rows),
            pl.BlockSpec((1, D), const),
            pl.BlockSpec((D, 2 * d_ff), const, pipeline_mode=single),
            pl.BlockSpec((FFN_CONV, 2 * d_ff), const),
            pl.BlockSpec((d_ff, D), const, pipeline_mode=single),
            pl.BlockSpec((1, D), const),
        ],
        out_specs=pl.BlockSpec((tm, D), rows),
        out_shape=jax.ShapeDtypeStruct((T, D), _f32),
        scratch_shapes=[
            pltpu.VMEM((tm + SUBLANES, 2 * tf), _f32),
            pltpu.VMEM((SUBLANES, 2 * d_ff), _f32),
            pltpu.VMEM((tm, d_ff), _bf16),
        ],
        compiler_params=_params(("arbitrary", "arbitrary")),
        name="ffn",
    )(h1, g, wu, conv_w, wd, gfin)


def _tiles(S):
    pick = lambda pref: next(t for t in pref if S % t == 0)
    return dict(
        tm_in=pick((1024, 512, 256, 128)),
        tq=pick((256, 128)),
        lc=pick((256, 128, 64)),
        tm_merge=pick((512, 256, 128)),
        tm_ffn=pick((512, 256, 128)),
    )


def kernel(x, norm_mix, w_in, fox_f_bias, gdn_conv_w, gdn_a_log, gdn_dt_bias, gdn_norm,
           w_branch_fox, w_branch_gdn, w_out, norm_ffn, w_up, ffn_conv_w, w_down, norm_final):
    B, S, D = x.shape
    L = norm_mix.shape[0]
    T = B * S
    assert D == WIDTH and S % LANES == 0
    t = _tiles(S)
    d_ff = w_down.shape[1]
    tf = 256 if d_ff % 256 == 0 else LANES
    h = x.reshape(T, D)
    o = 0
    seg = {}
    for name, width in (("fq", WIDTH), ("fk", WIDTH), ("fv", WIDTH), ("ff", HEADS),
                        ("gq", WIDTH), ("gk", WIDTH), ("gv", WIDTH), ("ga", HEADS), ("gb", HEADS),
                        ("gz", WIDTH), ("gate_fox", D), ("gate_gdn", D)):
        seg[name] = (o, o + width)
        o += width
    for l in range(L):
        w = w_in[l]
        cols = lambda n: w[:, seg[n][0]:seg[n][1]]
        w_big = jnp.concatenate(
            [cols("fq") * (HEAD_DIM ** -0.5), cols("fk"), cols("fv"), cols("gq"), cols("gk"), cols("gv"),
             cols("gz"), cols("gate_fox"), cols("gate_gdn")], axis=1).astype(_bf16)
        w_small = jnp.concatenate(
            [cols("ff"), cols("ga"), cols("gb"), jnp.zeros((D, LANES - 3 * HEADS), w.dtype)], axis=1).astype(_bf16)
        zpad = jnp.zeros((LANES - 2 * HEADS,), _f32)
        gate_params = jnp.zeros((SUBLANES, LANES), _f32)
        gate_params = gate_params.at[0].set(jnp.concatenate([fox_f_bias[l], gdn_dt_bias[l], zpad]))
        gate_params = gate_params.at[1].set(jnp.concatenate([jnp.zeros((HEADS,), _f32), gdn_a_log[l], zpad]))

        proj, small = _in_proj(h, norm_mix[l][None, :], w_big, w_small, tm=t["tm_in"], tn=N_COLBLOCKS * WIDTH // 4)
        g, negc = _gate_prep(small, gate_params, B, S)
        y_fox = _fox(proj, negc, B, S, tq=t["tq"])
        y_gdn = _gdn(proj, g, gdn_conv_w[l], gdn_norm[l][None, :], B, S, lc=t["lc"])
        h1 = _merge(h, y_fox, y_gdn, proj, w_branch_fox[l].astype(_bf16), w_branch_gdn[l].astype(_bf16),
                    w_out[l].astype(_bf16), tm=t["tm_merge"])
        h = _ffn(h1, norm_ffn[l][None, :], w_up[l].astype(_bf16), ffn_conv_w[l], w_down[l].astype(_bf16),
                 norm_final[None, :], B, S, tm=t["tm_ffn"], tf=tf, final_norm=(l == L - 1))
    if L == 0:
        raise ValueError("at least one layer expected")
    return h.reshape(B, S, D)
```

```python
import functools

import jax
import jax.numpy as jnp
from jax import lax
from jax.experimental import pallas as pl
from jax.experimental.pallas import tpu as pltpu

EPS = 1e-6
HEADS = 8
HEAD_DIM = 128
WIDTH = HEADS * HEAD_DIM
GDN_CONV = 4
GDN_CHUNK = 64
FFN_CONV = 3
LANES = 128
SUBLANES = 8
NEG_BIG = -1e30
LOG2E = 1.4426950408889634
FOX_HEADS_PER_STEP = 2

COL_FQ, COL_FK, COL_FV, COL_GQ, COL_GK, COL_GV, COL_GZ, COL_GATE_FOX, COL_GATE_GDN = range(9)
N_COLBLOCKS = 9
LANE_FF, LANE_GA, LANE_GB = 0, HEADS, 2 * HEADS

VMEM_LIMIT = 56 * 1024 * 1024

_f32 = jnp.float32
_bf16 = jnp.bfloat16


def _dot(a, b):
    return jnp.dot(a, b, preferred_element_type=_f32)


def _dot_nt(a, b):
    return lax.dot_general(a, b, (((1,), (1,)), ((), ())), preferred_element_type=_f32)


def _dot_tn(a, b):
    return lax.dot_general(a, b, (((0,), (0,)), ((), ())), preferred_element_type=_f32)


def _sigmoid(z):
    return 1.0 / (1.0 + jnp.exp(-z))


def _silu(z):
    return z * _sigmoid(z)


def _params(sem, vmem=VMEM_LIMIT):
    return pltpu.CompilerParams(dimension_semantics=sem, vmem_limit_bytes=vmem)


def _in_proj_kernel(x_ref, g_ref, wb_ref, ws_ref, proj_ref, small_ref, hn_ref, *, n_sub):
    j = pl.program_id(1)

    @pl.when(j == 0)
    def _():
        x = x_ref[...]
        ms = jnp.mean(x * x, axis=-1, keepdims=True)
        hn = (x * lax.rsqrt(ms + EPS) * g_ref[...]).astype(_bf16)
        hn_ref[...] = hn
        small_ref[...] = _dot(hn, ws_ref[...])

    hn = hn_ref[...]
    tn = wb_ref.shape[1]
    sub = tn // n_sub
    for c in range(n_sub):
        proj_ref[:, c * sub:(c + 1) * sub] = _dot(hn, wb_ref[:, c * sub:(c + 1) * sub]).astype(_bf16)


def _in_proj(x2, g, w_big, w_small, *, tm, tn):
    T, D = x2.shape
    N = w_big.shape[1]
    return pl.pallas_call(
        functools.partial(_in_proj_kernel, n_sub=tn // 768),
        grid=(T // tm, N // tn),
        in_specs=[
            pl.BlockSpec((tm, D), lambda i, j: (i, 0)),
            pl.BlockSpec((1, D), lambda i, j: (0, 0)),
            pl.BlockSpec((D, tn), lambda i, j: (0, j)),
            pl.BlockSpec((D, LANES), lambda i, j: (0, 0)),
        ],
        out_specs=[
            pl.BlockSpec((tm, tn), lambda i, j: (i, j)),
            pl.BlockSpec((tm, LANES), lambda i, j: (i, 0)),
        ],
        out_shape=[
            jax.ShapeDtypeStruct((T, N), _bf16),
            jax.ShapeDtypeStruct((T, LANES), _f32),
        ],
        scratch_shapes=[pltpu.VMEM((tm, D), _bf16)],
        compiler_params=_params(("arbitrary", "arbitrary")),
        name="in_proj",
    )(x2, g, w_big, w_small)


def _split3(v):
    hi = v.astype(_bf16)
    r1 = v - hi.astype(_f32)
    mid = r1.astype(_bf16)
    lo = (r1 - mid.astype(_f32)).astype(_bf16)
    return hi, mid, lo


def _gate_prep_kernel(raw_ref, p_ref, g_ref, nb_ref):
    S = raw_ref.shape[0]
    nblk = S // LANES
    lane = lax.broadcasted_iota(jnp.int32, (LANES, LANES), 1)
    row = lax.broadcasted_iota(jnp.int32, (LANES, LANES), 0)
    tri = (row >= lane)
    tri_full = jnp.where(tri, 1.0, 0.0).astype(_bf16)
    same_chunk = (row >= GDN_CHUNK) == (lane >= GDN_CHUNK)
    tri_chunk = jnp.where(tri & same_chunk, 1.0, 0.0).astype(_bf16)
    bias = p_ref[0:1, :]
    neg_a = -jnp.exp(p_ref[1:2, :])
    is_f = lane < LANE_GA
    is_g = lane < LANE_GB
    carry = jnp.zeros((1, LANES), _f32)
    for r in range(nblk):
        z = raw_ref[r * LANES:(r + 1) * LANES, :] + bias
        t = jnp.log(1.0 + jnp.exp(-jnp.abs(z)))
        logsig = jnp.minimum(z, 0.0) - t
        softplus = jnp.maximum(z, 0.0) + t
        sig = _sigmoid(z)
        vals = jnp.where(is_f, logsig, jnp.where(is_g, neg_a * softplus, sig))
        hi, mid, lo = _split3(vals)
        cum_full = (_dot(tri_full, hi) + _dot(tri_full, mid)) + _dot(tri_full, lo) + carry
        cum_chunk = (_dot(tri_chunk, hi) + _dot(tri_chunk, mid)) + _dot(tri_chunk, lo)
        carry = cum_full[LANES - 1:LANES, :]
        out = jnp.where(is_f, cum_full, jnp.where(is_g, cum_chunk, vals))
        g_ref[r * LANES:(r + 1) * LANES, :] = out
        for h in range(HEADS):
            nb_ref[h, r * LANES:(r + 1) * LANES, :] = jnp.broadcast_to(
                out[:, LANE_FF + h:LANE_FF + h + 1] * (-LOG2E), (LANES, LANES))


def _gate_prep(small, gate_params, B, S):
    T = small.shape[0]
    return pl.pallas_call(
        _gate_prep_kernel,
        grid=(B,),
        in_specs=[
            pl.BlockSpec((S, LANES), lambda b: (b, 0)),
            pl.BlockSpec((SUBLANES, LANES), lambda b: (0, 0)),
        ],
        out_specs=[
            pl.BlockSpec((S, LANES), lambda b: (b, 0)),
            pl.BlockSpec((None, HEADS, S, LANES), lambda b: (b, 0, 0, 0)),
        ],
        out_shape=[
            jax.ShapeDtypeStruct((T, LANES), _f32),
            jax.ShapeDtypeStruct((B, HEADS, S, LANES), _f32),
        ],
        compiler_params=_params(("arbitrary",)),
        name="gate_prep",
    )(small, gate_params)


def _fox_kernel(q_ref, k_ref, v_ref, nb_ref, o_ref, vt_ref, *, tq):
    S = q_ref.shape[0]
    nq = S // tq
    hs = range(FOX_HEADS_PER_STEP)
    ri = lax.broadcasted_iota(jnp.int32, (tq, tq), 0)
    ci = lax.broadcasted_iota(jnp.int32, (tq, tq), 1)
    visible = ri <= ci
    eye = jnp.where(lax.broadcasted_iota(jnp.int32, (HEAD_DIM, HEAD_DIM), 0)
                    == lax.broadcasted_iota(jnp.int32, (HEAD_DIM, HEAD_DIM), 1), 1.0, 0.0).astype(_bf16)
    cols = lambda h: slice(h * HEAD_DIM, (h + 1) * HEAD_DIM)
    rows = lambda j: slice(j * tq, (j + 1) * tq)
    for j in range(nq):
        for h in hs:
            vt_ref[h, :, rows(j)] = _dot_nt(eye, v_ref[rows(j), cols(h)]).astype(_bf16)

    def score(i, j, h):
        s = _dot_nt(k_ref[rows(j), cols(h)], q_ref[rows(i), cols(h)])
        nb = nb_ref[h, rows(j), :]
        s = s + jnp.concatenate([nb] * (tq // LANES), axis=-1)
        if i == j:
            s = jnp.where(visible, s, NEG_BIG)
        return s

    tiles = [(i, j) for i in range(nq) for j in range(i + 1)]
    s_next = [score(*tiles[0], h) for h in hs]
    for t, (i, j) in enumerate(tiles):
        s = s_next
        if t + 1 < len(tiles):
            s_next = [score(*tiles[t + 1], h) for h in hs]
        if j == 0:
            m = [jnp.full((1, tq), NEG_BIG, _f32) for h in hs]
            l = [jnp.zeros((1, tq), _f32) for h in hs]
            acc = [jnp.zeros((HEAD_DIM, tq), _f32) for h in hs]
        m_new = [jnp.maximum(m[h], jnp.max(s[h], axis=0, keepdims=True)) for h in hs]
        alpha = [jnp.exp2(m[h] - m_new[h]) for h in hs]
        p = [jnp.exp2(s[h] - m_new[h]) for h in hs]
        l = [alpha[h] * l[h] + jnp.sum(p[h], axis=0, keepdims=True) for h in hs]
        acc = [alpha[h] * acc[h] + _dot(vt_ref[h, :, rows(j)], p[h].astype(_bf16)) for h in hs]
        m = m_new
        if j == i:
            for h in hs:
                o_ref[rows(i), cols(h)] = (acc[h] * (1.0 / l[h])).T.astype(o_ref.dtype)


def _fox(proj, negc, B, S, *, tq):
    T = proj.shape[0]
    hp = FOX_HEADS_PER_STEP
    width = hp * HEAD_DIM
    return pl.pallas_call(
        functools.partial(_fox_kernel, tq=tq),
        grid=(B, HEADS // hp),
        in_specs=[
            pl.BlockSpec((S, width), lambda b, h: (b, COL_FQ * (HEADS // hp) + h)),
            pl.BlockSpec((S, width), lambda b, h: (b, COL_FK * (HEADS // hp) + h)),
            pl.BlockSpec((S, width), lambda b, h: (b, COL_FV * (HEADS // hp) + h)),
            pl.BlockSpec((None, hp, S, LANES), lambda b, h: (b, h, 0, 0)),
        ],
        out_specs=pl.BlockSpec((S, width), lambda b, h: (b, h)),
        out_shape=jax.ShapeDtypeStruct((T, WIDTH), _bf16),
        scratch_shapes=[pltpu.VMEM((hp, HEAD_DIM, S), _bf16)],
        compiler_params=_params(("arbitrary", "arbitrary")),
        name="fox",
    )(proj, proj, proj, negc)


def _gdn_kernel(q_ref, k_ref, v_ref, z_ref, g_ref, cw_ref, nw_ref, o_ref,
                xbuf_ref, cbuf_ref, state_ref, *, lc):
    C = GDN_CHUNK
    step = pl.program_id(1)

    @pl.when(step == 0)
    def _():
        xbuf_ref[0:SUBLANES, :] = jnp.zeros((SUBLANES, 3 * WIDTH), _f32)
        state_ref[...] = jnp.zeros(state_ref.shape, _f32)

    xbuf_ref[SUBLANES:, 0:WIDTH] = q_ref[...].astype(_f32)
    xbuf_ref[SUBLANES:, WIDTH:2 * WIDTH] = k_ref[...].astype(_f32)
    xbuf_ref[SUBLANES:, 2 * WIDTH:] = v_ref[...].astype(_f32)

    ri = lax.broadcasted_iota(jnp.int32, (C, C), 0)
    ci = lax.broadcasted_iota(jnp.int32, (C, C), 1)
    incl = ci <= ri
    strict = ci < ri
    scale = HEAD_DIM ** -0.5
    nw = nw_ref[...]

    def l2n(t):
        return t * lax.rsqrt(jnp.sum(t * t, axis=-1, keepdims=True) + EPS)

    for cb in range(3 * HEADS):
        col = cb * HEAD_DIM
        acc = None
        for i in range(GDN_CONV):
            off = SUBLANES - (GDN_CONV - 1) + i
            term = xbuf_ref[off:off + lc, col:col + HEAD_DIM] * cw_ref[i:i + 1, col:col + HEAD_DIM]
            acc = term if acc is None else acc + term
        act = _silu(acc)
        cbuf_ref[:, col:col + HEAD_DIM] = l2n(act) if cb < 2 * HEADS else act

    def pair_body(n, carry):
        p0 = pl.multiple_of(n * LANES, LANES)
        gpair = g_ref[pl.ds(p0, LANES), :]
        gpair_t = gpair.T
        nc = LANES // C
        items = [(c, h) for c in range(nc) for h in range(HEADS)]
        it = range(len(items))
        rows = [pl.ds(p0 + c * C, C) for c, h in items]
        hcol = [slice(h * HEAD_DIM, (h + 1) * HEAD_DIM) for c, h in items]
        q = [cbuf_ref[rows[x], hcol[x]] for x in it]
        k = [cbuf_ref[rows[x], WIDTH + items[x][1] * HEAD_DIM:WIDTH + (items[x][1] + 1) * HEAD_DIM] for x in it]
        v = [cbuf_ref[rows[x], 2 * WIDTH + items[x][1] * HEAD_DIM:2 * WIDTH + (items[x][1] + 1) * HEAD_DIM]
             for x in it]
        gcol = [gpair[c * C:(c + 1) * C, LANE_GA + h:LANE_GA + h + 1] for c, h in items]
        bcol = [gpair[c * C:(c + 1) * C, LANE_GB + h:LANE_GB + h + 1] for c, h in items]
        grow = [gpair_t[LANE_GA + h:LANE_GA + h + 1, c * C:(c + 1) * C] for c, h in items]
        glast = [gcol[x][C - 1:C, :] for x in it]
        decay = [jnp.exp(jnp.where(incl, gcol[x] - grow[x], NEG_BIG)) for x in it]
        e_col = [jnp.exp(gcol[x]) for x in it]
        kb = [k[x] * bcol[x] for x in it]
        k16 = [k[x].astype(_bf16) for x in it]
        a = [jnp.where(strict, _dot_nt(kb[x].astype(_bf16), k16[x]) * decay[x], 0.0) for x in it]
        attn = [(_dot_nt((q[x] * scale).astype(_bf16), k16[x]) * decay[x]).astype(_bf16) for x in it]
        rhs = [jnp.concatenate([v[x] * bcol[x], kb[x] * e_col[x]], axis=-1).astype(_bf16) for x in it]
        q_dec = [(q[x] * (scale * e_col[x])).astype(_bf16) for x in it]
        k_dec = [(k[x] * jnp.exp(glast[x] - gcol[x])).astype(_bf16) for x in it]
        xp = [-a[x] for x in it]
        r = xp
        for _ in range(5):
            x16 = [xp[x].astype(_bf16) for x in it]
            xp = [_dot(x16[x], x16[x]) for x in it]
            r = [r[x] + xp[x] + _dot(r[x].astype(_bf16), xp[x].astype(_bf16)) for x in it]
        sol = [rhs[x].astype(_f32) + _dot(r[x].astype(_bf16), rhs[x]) for x in it]
        u_hat = [sol[x][:, :HEAD_DIM] for x in it]
        w16 = [sol[x][:, HEAD_DIM:].astype(_bf16) for x in it]
        hs = range(HEADS)
        st = [state_ref[h] for h in hs]
        for c in range(nc):
            xs = [c * HEADS + h for h in hs]
            st16 = [st[h].astype(_bf16) for h in hs]
            u = [u_hat[xs[h]] - _dot(w16[xs[h]], st16[h]) for h in hs]
            u16 = [u[h].astype(_bf16) for h in hs]
            o = [_dot(q_dec[xs[h]], st16[h]) + _dot(attn[xs[h]], u16[h]) for h in hs]
            st = [st[h] * jnp.exp(glast[xs[h]]) + _dot_tn(k_dec[xs[h]], u16[h]) for h in hs]
            for h in hs:
                x = xs[h]
                on = o[h] * lax.rsqrt(jnp.mean(o[h] * o[h], axis=-1, keepdims=True) + EPS) * nw
                zg = z_ref[rows[x], hcol[x]].astype(_f32)
                o_ref[rows[x], hcol[x]] = (on * _silu(zg)).astype(o_ref.dtype)
        for h in hs:
            state_ref[h] = st[h]
        return carry

    lax.fori_loop(0, lc // LANES, pair_body, 0)
    xbuf_ref[0:SUBLANES, :] = xbuf_ref[lc:lc + SUBLANES, :]


def _gdn(proj, g, conv_w, norm_w, B, S, *, lc):
    T = proj.shape[0]
    nsteps = S // lc
    row = lambda b, s: b * nsteps + s
    return pl.pallas_call(
        functools.partial(_gdn_kernel, lc=lc),
        grid=(B, nsteps),
        in_specs=[
            pl.BlockSpec((lc, WIDTH), lambda b, s: (row(b, s), COL_GQ)),
            pl.BlockSpec((lc, WIDTH), lambda b, s: (row(b, s), COL_GK)),
            pl.BlockSpec((lc, WIDTH), lambda b, s: (row(b, s), COL_GV)),
            pl.BlockSpec((lc, WIDTH), lambda b, s: (row(b, s), COL_GZ)),
            pl.BlockSpec((lc, LANES), lambda b, s: (row(b, s), 0)),
            pl.BlockSpec((GDN_CONV, 3 * WIDTH), lambda b, s: (0, 0)),
            pl.BlockSpec((1, HEAD_DIM), lambda b, s: (0, 0)),
        ],
        out_specs=pl.BlockSpec((lc, WIDTH), lambda b, s: (row(b, s), 0)),
        out_shape=jax.ShapeDtypeStruct((T, WIDTH), _bf16),
        scratch_shapes=[
            pltpu.VMEM((lc + SUBLANES, 3 * WIDTH), _f32),
            pltpu.VMEM((lc, 3 * WIDTH), _f32),
            pltpu.VMEM((HEADS, HEAD_DIM, HEAD_DIM), _f32),
        ],
        compiler_params=_params(("arbitrary", "arbitrary")),
        name="gdn",
    )(proj, proj, proj, proj, g, conv_w, norm_w)


def _merge_kernel(x_ref, yf_ref, yg_ref, gf_ref, gg_ref, wf_ref, wg_ref, wo_ref, o_ref):
    a = _sigmoid(gf_ref[...].astype(_f32)) * _dot(yf_ref[...], wf_ref[...])
    b = _sigmoid(gg_ref[...].astype(_f32)) * _dot(yg_ref[...], wg_ref[...])
    y = (a + b).astype(_bf16)
    o_ref[...] = x_ref[...] + _dot(y, wo_ref[...])


def _merge(x2, y_fox, y_gdn, proj, wf, wg, wo, *, tm):
    T, D = x2.shape
    rows = lambda i: (i, 0)
    const = lambda i: (0, 0)
    return pl.pallas_call(
        _merge_kernel,
        grid=(T // tm,),
        in_specs=[
            pl.BlockSpec((tm, D), rows),
            pl.BlockSpec((tm, WIDTH), rows),
            pl.BlockSpec((tm, WIDTH), rows),
            pl.BlockSpec((tm, D), lambda i: (i, COL_GATE_FOX)),
            pl.BlockSpec((tm, D), lambda i: (i, COL_GATE_GDN)),
            pl.BlockSpec((WIDTH, D), const),
            pl.BlockSpec((WIDTH, D), const),
            pl.BlockSpec((D, D), const),
        ],
        out_specs=pl.BlockSpec((tm, D), rows),
        out_shape=jax.ShapeDtypeStruct((T, D), _f32),
        compiler_params=_params(("arbitrary",)),
        name="merge",
    )(x2, y_fox, y_gdn, proj, proj, wf, wg, wo)


def _ffn_kernel(h_ref, g_ref, wu_ref, cw_ref, wd_ref, gfin_ref, o_ref,
                ubuf_ref, tail_ref, act_ref, *, tf, final_norm):
    tm = h_ref.shape[0]
    d_ff = wd_ref.shape[0]
    step = pl.program_id(1)

    @pl.when(step == 0)
    def _():
        tail_ref[...] = jnp.zeros(tail_ref.shape, _f32)

    x = h_ref[...]
    hn = (x * lax.rsqrt(jnp.mean(x * x, axis=-1, keepdims=True) + EPS) * g_ref[...]).astype(_bf16)

    def conv(c0, half):
        up = _dot(hn, wu_ref[:, c0:c0 + tf])
        lo = half * tf
        ubuf_ref[0:SUBLANES, lo:lo + tf] = tail_ref[:, c0:c0 + tf]
        ubuf_ref[SUBLANES:, lo:lo + tf] = up
        tail_ref[:, c0:c0 + tf] = up[tm - SUBLANES:, :]
        acc = None
        for i in range(FFN_CONV):
            off = SUBLANES - (FFN_CONV - 1) + i
            term = ubuf_ref[off:off + tm, lo:lo + tf] * cw_ref[i:i + 1, c0:c0 + tf]
            acc = term if acc is None else acc + term
        return acc

    for c in range(d_ff // tf):
        gate = conv(c * tf, 0)
        val = conv(d_ff + c * tf, 1)
        act_ref[:, c * tf:(c + 1) * tf] = (_silu(gate) * val).astype(_bf16)

    h2 = x + _dot(act_ref[...], wd_ref[...])
    if final_norm:
        h2 = h2 * lax.rsqrt(jnp.mean(h2 * h2, axis=-1, keepdims=True) + EPS) * gfin_ref[...]
    o_ref[...] = h2


def _ffn(h1, g, wu, conv_w, wd, gfin, B, S, *, tm, tf, final_norm):
    T, D = h1.shape
    d_ff = wd.shape[0]
    nsteps = S // tm
    rows = lambda b, s: (b * nsteps + s, 0)
    const = lambda b, s: (0, 0)
    single = pl.Buffered(1)
    return pl.pallas_call(
        functools.partial(_ffn_kernel, tf=tf, final_norm=final_norm),
        grid=(B, nsteps),
        in_specs=[
            pl.BlockSpec((tm, D), rows),
            pl.BlockSpec((1, D), const),
            pl.BlockSpec((D, 2 * d_ff), const, pipeline_mode=single),
            pl.BlockSpec((FFN_CONV, 2 * d_ff), const),
            pl.BlockSpec((d_ff, D), const, pipeline_mode=single),
            pl.BlockSpec((1, D), const),
        ],
        out_specs=pl.BlockSpec((tm, D), rows),
        out_shape=jax.ShapeDtypeStruct((T, D), _f32),
        scratch_shapes=[
            pltpu.VMEM((tm + SUBLANES, 2 * tf), _f32),
            pltpu.VMEM((SUBLANES, 2 * d_ff), _f32),
            pltpu.VMEM((tm, d_ff), _bf16),
        ],
        compiler_params=_params(("arbitrary", "arbitrary")),
        name="ffn",
    )(h1, g, wu, conv_w, wd, gfin)


def _tiles(S):
    pick = lambda pref: next(t for t in pref if S % t == 0)
    return dict(
        tm_in=pick((1024, 512, 256, 128)),
        tq=pick((256, 128)),
        lc=pick((256, 128, 64)),
        tm_merge=pick((512, 256, 128)),
        tm_ffn=pick((512, 256, 128)),
    )


def kernel(x, norm_mix, w_in, fox_f_bias, gdn_conv_w, gdn_a_log, gdn_dt_bias, gdn_norm,
           w_branch_fox, w_branch_gdn, w_out, norm_ffn, w_up, ffn_conv_w, w_down, norm_final):
    B, S, D = x.shape
    L = norm_mix.shape[0]
    T = B * S
    assert D == WIDTH and S % LANES == 0
    t = _tiles(S)
    d_ff = w_down.shape[1]
    tf = 256 if d_ff % 256 == 0 else LANES
    h = x.reshape(T, D)
    o = 0
    seg = {}
    for name, width in (("fq", WIDTH), ("fk", WIDTH), ("fv", WIDTH), ("ff", HEADS),
                        ("gq", WIDTH), ("gk", WIDTH), ("gv", WIDTH), ("ga", HEADS), ("gb", HEADS),
                        ("gz", WIDTH), ("gate_fox", D), ("gate_gdn", D)):
        seg[name] = (o, o + width)
        o += width
    for l in range(L):
        w = w_in[l]
        cols = lambda n: w[:, seg[n][0]:seg[n][1]]
        w_big = jnp.concatenate(
            [cols("fq") * (HEAD_DIM ** -0.5 * LOG2E), cols("fk"), cols("fv"), cols("gq"), cols("gk"), cols("gv"),
             cols("gz"), cols("gate_fox"), cols("gate_gdn")], axis=1).astype(_bf16)
        w_small = jnp.concatenate(
            [cols("ff"), cols("ga"), cols("gb"), jnp.zeros((D, LANES - 3 * HEADS), w.dtype)], axis=1).astype(_bf16)
        zpad = jnp.zeros((LANES - 2 * HEADS,), _f32)
        gate_params = jnp.zeros((SUBLANES, LANES), _f32)
        gate_params = gate_params.at[0].set(jnp.concatenate([fox_f_bias[l], gdn_dt_bias[l], zpad]))
        gate_params = gate_params.at[1].set(jnp.concatenate([jnp.zeros((HEADS,), _f32), gdn_a_log[l], zpad]))

        proj, small = _in_proj(h, norm_mix[l][None, :], w_big, w_small, tm=t["tm_in"], tn=N_COLBLOCKS * WIDTH // 4)
        g, negc = _gate_prep(small, gate_params, B, S)
        y_fox = _fox(proj, negc, B, S, tq=t["tq"])
        y_gdn = _gdn(proj, g, gdn_conv_w[l], gdn_norm[l][None, :], B, S, lc=t["lc"])
        h1 = _merge(h, y_fox, y_gdn, proj, w_branch_fox[l].astype(_bf16), w_branch_gdn[l].astype(_bf16),
                    w_out[l].astype(_bf16), tm=t["tm_merge"])
        h = _ffn(h1, norm_ffn[l][None, :], w_up[l].astype(_bf16), ffn_conv_w[l], w_down[l].astype(_bf16),
                 norm_final[None, :], B, S, tm=t["tm_ffn"], tf=tf, final_norm=(l == L - 1))
    if L == 0:
        raise ValueError("at least one layer expected")
    return h.reshape(B, S, D)
```

```python
import functools

import jax
import jax.numpy as jnp
from jax import lax
from jax.experimental import pallas as pl
from jax.experimental.pallas import tpu as pltpu

EPS = 1e-6
HEADS = 8
HEAD_DIM = 128
WIDTH = HEADS * HEAD_DIM
GDN_CONV = 4
GDN_CHUNK = 64
FFN_CONV = 3
LANES = 128
SUBLANES = 8
NEG_BIG = -1e30
LOG2E = 1.4426950408889634
FOX_HEADS_PER_STEP = 2

COL_FQ, COL_FK, COL_FV, COL_GQ, COL_GK, COL_GV, COL_GZ, COL_GATE_FOX, COL_GATE_GDN = range(9)
N_COLBLOCKS = 9
N_COLSTEPS = 3
COLSTEP_GDN = 1
IN_PROJ_SUB = 768
CONV_SUB = 256
LANE_FF, LANE_GA, LANE_GB = 0, HEADS, 2 * HEADS

VMEM_LIMIT = 56 * 1024 * 1024

_f32 = jnp.float32
_bf16 = jnp.bfloat16


def _dot(a, b):
    return jnp.dot(a, b, preferred_element_type=_f32)


def _dot_nt(a, b):
    return lax.dot_general(a, b, (((1,), (1,)), ((), ())), preferred_element_type=_f32)


def _dot_tn(a, b):
    return lax.dot_general(a, b, (((0,), (0,)), ((), ())), preferred_element_type=_f32)


def _sigmoid(z):
    return 1.0 / (1.0 + jnp.exp(-z))


def _silu(z):
    return z * _sigmoid(z)


def _params(sem, vmem=VMEM_LIMIT):
    return pltpu.CompilerParams(dimension_semantics=sem, vmem_limit_bytes=vmem)


def _in_proj_kernel(x_ref, g_ref, wb_ref, ws_ref, cw_ref, proj_ref, small_ref, hn_ref, ubuf_ref, tail_ref,
                    *, tiles_per_seq):
    i = pl.program_id(0)
    j = pl.program_id(1)
    tm = x_ref.shape[0]
    tn = wb_ref.shape[1]

    @pl.when(j == 0)
    def _():
        x = x_ref[...]
        ms = jnp.mean(x * x, axis=-1, keepdims=True)
        hn = (x * lax.rsqrt(ms + EPS) * g_ref[...]).astype(_bf16)
        hn_ref[...] = hn
        small_ref[...] = _dot(hn, ws_ref[...])

    @pl.when(j != COLSTEP_GDN)
    def _():
        hn = hn_ref[...]
        for c0 in range(0, tn, IN_PROJ_SUB):
            proj_ref[:, c0:c0 + IN_PROJ_SUB] = _dot(hn, wb_ref[:, c0:c0 + IN_PROJ_SUB]).astype(_bf16)

    @pl.when((j == COLSTEP_GDN) & (lax.rem(i, tiles_per_seq) == 0))
    def _():
        tail_ref[...] = jnp.zeros(tail_ref.shape, _f32)

    @pl.when(j == COLSTEP_GDN)
    def _():
        hn = hn_ref[...]
        w = CONV_SUB
        for c in range(tn // w):
            c0 = c * w
            lo = (c % 2) * w
            up = _dot(hn, wb_ref[:, c0:c0 + w])
            ubuf_ref[0:SUBLANES, lo:lo + w] = tail_ref[:, c0:c0 + w]
            ubuf_ref[SUBLANES:, lo:lo + w] = up
            tail_ref[:, c0:c0 + w] = up[tm - SUBLANES:, :]
            acc = None
            for t in range(GDN_CONV):
                off = SUBLANES - (GDN_CONV - 1) + t
                term = ubuf_ref[off:off + tm, lo:lo + w] * cw_ref[t:t + 1, c0:c0 + w]
                acc = term if acc is None else acc + term
            proj_ref[:, c0:c0 + w] = acc.astype(_bf16)


def _in_proj(x2, g, w_big, w_small, conv_w, S, *, tm):
    T, D = x2.shape
    N = w_big.shape[1]
    tn = N // N_COLSTEPS
    return pl.pallas_call(
        functools.partial(_in_proj_kernel, tiles_per_seq=S // tm),
        grid=(T // tm, N_COLSTEPS),
        in_specs=[
            pl.BlockSpec((tm, D), lambda i, j: (i, 0)),
            pl.BlockSpec((1, D), lambda i, j: (0, 0)),
            pl.BlockSpec((D, tn), lambda i, j: (0, j)),
            pl.BlockSpec((D, LANES), lambda i, j: (0, 0)),
            pl.BlockSpec((GDN_CONV, tn), lambda i, j: (0, 0)),
        ],
        out_specs=[
            pl.BlockSpec((tm, tn), lambda i, j: (i, j)),
            pl.BlockSpec((tm, LANES), lambda i, j: (i, 0)),
        ],
        out_shape=[
            jax.ShapeDtypeStruct((T, N), _bf16),
            jax.ShapeDtypeStruct((T, LANES), _f32),
        ],
        scratch_shapes=[
            pltpu.VMEM((tm, D), _bf16),
            pltpu.VMEM((tm + SUBLANES, 2 * CONV_SUB), _f32),
            pltpu.VMEM((SUBLANES, tn), _f32),
        ],
        compiler_params=_params(("arbitrary", "arbitrary")),
        name="in_proj",
    )(x2, g, w_big, w_small, conv_w)


def _split3(v):
    hi = v.astype(_bf16)
    r1 = v - hi.astype(_f32)
    mid = r1.astype(_bf16)
    lo = (r1 - mid.astype(_f32)).astype(_bf16)
    return hi, mid, lo


def _gate_prep_kernel(raw_ref, p_ref, g_ref, nb_ref):
    S = raw_ref.shape[0]
    nblk = S // LANES
    lane = lax.broadcasted_iota(jnp.int32, (LANES, LANES), 1)
    row = lax.broadcasted_iota(jnp.int32, (LANES, LANES), 0)
    tri = (row >= lane)
    tri_full = jnp.where(tri, 1.0, 0.0).astype(_bf16)
    same_chunk = (row >= GDN_CHUNK) == (lane >= GDN_CHUNK)
    tri_chunk = jnp.where(tri & same_chunk, 1.0, 0.0).astype(_bf16)
    bias = p_ref[0:1, :]
    neg_a = -jnp.exp(p_ref[1:2, :])
    is_f = lane < LANE_GA
    is_g = lane < LANE_GB
    carry = jnp.zeros((1, LANES), _f32)
    for r in range(nblk):
        z = raw_ref[r * LANES:(r + 1) * LANES, :] + bias
        t = jnp.log(1.0 + jnp.exp(-jnp.abs(z)))
        logsig = jnp.minimum(z, 0.0) - t
        softplus = jnp.maximum(z, 0.0) + t
        sig = _sigmoid(z)
        vals = jnp.where(is_f, logsig, jnp.where(is_g, neg_a * softplus, sig))
        hi, mid, lo = _split3(vals)
        cum_full = (_dot(tri_full, hi) + _dot(tri_full, mid)) + _dot(tri_full, lo) + carry
        cum_chunk = (_dot(tri_chunk, hi) + _dot(tri_chunk, mid)) + _dot(tri_chunk, lo)
        carry = cum_full[LANES - 1:LANES, :]
        out = jnp.where(is_f, cum_full, jnp.where(is_g, cum_chunk, vals))
        g_ref[r * LANES:(r + 1) * LANES, :] = out
        for h in range(HEADS):
            nb_ref[h, r * LANES:(r + 1) * LANES, :] = jnp.broadcast_to(
                out[:, LANE_FF + h:LANE_FF + h + 1] * (-LOG2E), (LANES, LANES))


def _gate_prep(small, gate_params, B, S):
    T = small.shape[0]
    return pl.pallas_call(
        _gate_prep_kernel,
        grid=(B,),
        in_specs=[
            pl.BlockSpec((S, LANES), lambda b: (b, 0)),
            pl.BlockSpec((SUBLANES, LANES), lambda b: (0, 0)),
        ],
        out_specs=[
            pl.BlockSpec((S, LANES), lambda b: (b, 0)),
            pl.BlockSpec((None, HEADS, S, LANES), lambda b: (b, 0, 0, 0)),
        ],
        out_shape=[
            jax.ShapeDtypeStruct((T, LANES), _f32),
            jax.ShapeDtypeStruct((B, HEADS, S, LANES), _f32),
        ],
        compiler_params=_params(("arbitrary",)),
        name="gate_prep",
    )(small, gate_params)


def _fox_kernel(q_ref, k_ref, v_ref, nb_ref, o_ref, vt_ref, *, tq):
    S = q_ref.shape[0]
    nq = S // tq
    hs = range(FOX_HEADS_PER_STEP)
    ri = lax.broadcasted_iota(jnp.int32, (tq, tq), 0)
    ci = lax.broadcasted_iota(jnp.int32, (tq, tq), 1)
    visible = ri <= ci
    eye = jnp.where(lax.broadcasted_iota(jnp.int32, (HEAD_DIM, HEAD_DIM), 0)
                    == lax.broadcasted_iota(jnp.int32, (HEAD_DIM, HEAD_DIM), 1), 1.0, 0.0).astype(_bf16)
    cols = lambda h: slice(h * HEAD_DIM, (h + 1) * HEAD_DIM)
    rows = lambda j: slice(j * tq, (j + 1) * tq)
    for j in range(nq):
        for h in hs:
            vt_ref[h, :, rows(j)] = _dot_nt(eye, v_ref[rows(j), cols(h)]).astype(_bf16)

    def score(i, j, h):
        s = _dot_nt(k_ref[rows(j), cols(h)], q_ref[rows(i), cols(h)])
        nb = nb_ref[h, rows(j), :]
        s = s + jnp.concatenate([nb] * (tq // LANES), axis=-1)
        if i == j:
            s = jnp.where(visible, s, NEG_BIG)
        return s

    tiles = [(i, j) for i in range(nq) for j in range(i + 1)]
    s_next = [score(*tiles[0], h) for h in hs]
    for t, (i, j) in enumerate(tiles):
        s = s_next
        if t + 1 < len(tiles):
            s_next = [score(*tiles[t + 1], h) for h in hs]
        if j == 0:
            m = [jnp.full((1, tq), NEG_BIG, _f32) for h in hs]
            l = [jnp.zeros((1, tq), _f32) for h in hs]
            acc = [jnp.zeros((HEAD_DIM, tq), _f32) for h in hs]
        m_new = [jnp.maximum(m[h], jnp.max(s[h], axis=0, keepdims=True)) for h in hs]
        alpha = [jnp.exp2(m[h] - m_new[h]) for h in hs]
        p = [jnp.exp2(s[h] - m_new[h]) for h in hs]
        l = [alpha[h] * l[h] + jnp.sum(p[h], axis=0, keepdims=True) for h in hs]
        acc = [alpha[h] * acc[h] + _dot(vt_ref[h, :, rows(j)], p[h].astype(_bf16)) for h in hs]
        m = m_new
        if j == i:
            for h in hs:
                o_ref[rows(i), cols(h)] = (acc[h] * (1.0 / l[h])).T.astype(o_ref.dtype)


def _fox(proj, negc, B, S, *, tq):
    T = proj.shape[0]
    hp = FOX_HEADS_PER_STEP
    width = hp * HEAD_DIM
    return pl.pallas_call(
        functools.partial(_fox_kernel, tq=tq),
        grid=(B, HEADS // hp),
        in_specs=[
            pl.BlockSpec((S, width), lambda b, h: (b, COL_FQ * (HEADS // hp) + h)),
            pl.BlockSpec((S, width), lambda b, h: (b, COL_FK * (HEADS // hp) + h)),
            pl.BlockSpec((S, width), lambda b, h: (b, COL_FV * (HEADS // hp) + h)),
            pl.BlockSpec((None, hp, S, LANES), lambda b, h: (b, h, 0, 0)),
        ],
        out_specs=pl.BlockSpec((S, width), lambda b, h: (b, h)),
        out_shape=jax.ShapeDtypeStruct((T, WIDTH), _bf16),
        scratch_shapes=[pltpu.VMEM((hp, HEAD_DIM, S), _bf16)],
        compiler_params=_params(("arbitrary", "arbitrary")),
        name="fox",
    )(proj, proj, proj, negc)


def _gdn_kernel(q_ref, k_ref, v_ref, z_ref, g_ref, nw_ref, o_ref, cbuf_ref, state_ref, *, lc):
    C = GDN_CHUNK
    P = LANES
    npairs = lc // P
    nc = P // C
    step = pl.program_id(1)
    srcs = (q_ref, k_ref, v_ref)

    @pl.when(step == 0)
    def _():
        state_ref[...] = jnp.zeros(state_ref.shape, _f32)

    ri = lax.broadcasted_iota(jnp.int32, (C, C), 0)
    ci = lax.broadcasted_iota(jnp.int32, (C, C), 1)
    incl = ci <= ri
    strict = ci < ri
    scale = HEAD_DIM ** -0.5
    nw = nw_ref[...]

    def act_unit(cb, p1, slot):
        c0 = (cb % HEADS) * HEAD_DIM
        act = _silu(srcs[cb // HEADS][pl.ds(p1, P), c0:c0 + HEAD_DIM].astype(_f32))
        if cb < 2 * HEADS:
            act = act * lax.rsqrt(jnp.sum(act * act, axis=-1, keepdims=True) + EPS)
        cbuf_ref[slot, :, cb * HEAD_DIM:(cb + 1) * HEAD_DIM] = act

    for cb in range(3 * HEADS):
        act_unit(cb, 0, 0)

    def pair_body(n, carry):
        p0 = pl.multiple_of(n * P, P)
        slot = lax.rem(n, 2)
        p1 = pl.multiple_of(jnp.minimum(n + 1, npairs - 1) * P, P)
        pending = iter(range(3 * HEADS))

        def emit(count):
            for _ in range(count):
                cb = next(pending, None)
                if cb is not None:
                    act_unit(cb, p1, 1 - slot)

        gpair = g_ref[pl.ds(p0, P), :]
        gpair_t = gpair.T
        items = [(c, h) for c in range(nc) for h in range(HEADS)]
        it = range(len(items))
        crow = [slice(c * C, (c + 1) * C) for c, h in items]
        hcol = [slice(h * HEAD_DIM, (h + 1) * HEAD_DIM) for c, h in items]
        q = [cbuf_ref[slot, crow[x], hcol[x]] for x in it]
        k = [cbuf_ref[slot, crow[x], WIDTH + items[x][1] * HEAD_DIM:WIDTH + (items[x][1] + 1) * HEAD_DIM]
             for x in it]
        v = [cbuf_ref[slot, crow[x], 2 * WIDTH + items[x][1] * HEAD_DIM:2 * WIDTH + (items[x][1] + 1) * HEAD_DIM]
             for x in it]
        gcol = [gpair[c * C:(c + 1) * C, LANE_GA + h:LANE_GA + h + 1] for c, h in items]
        bcol = [gpair[c * C:(c + 1) * C, LANE_GB + h:LANE_GB + h + 1] for c, h in items]
        grow = [gpair_t[LANE_GA + h:LANE_GA + h + 1, c * C:(c + 1) * C] for c, h in items]
        glast = [gcol[x][C - 1:C, :] for x in it]
        decay = [jnp.exp(jnp.where(incl, gcol[x] - grow[x], NEG_BIG)) for x in it]
        e_col = [jnp.exp(gcol[x]) for x in it]
        emit(2)
        kb = [k[x] * bcol[x] for x in it]
        k16 = [k[x].astype(_bf16) for x in it]
        a = [jnp.where(strict, _dot_nt(kb[x].astype(_bf16), k16[x]) * decay[x], 0.0) for x in it]
        attn = [(_dot_nt((q[x] * scale).astype(_bf16), k16[x]) * decay[x]).astype(_bf16) for x in it]
        emit(2)
        rhs = [jnp.concatenate([v[x] * bcol[x], kb[x] * e_col[x]], axis=-1).astype(_bf16) for x in it]
        q_dec = [(q[x] * (scale * e_col[x])).astype(_bf16) for x in it]
        k_dec = [(k[x] * jnp.exp(glast[x] - gcol[x])).astype(_bf16) for x in it]
        emit(2)
        xp = [-a[x] for x in it]
        r = xp
        for _ in range(5):
            x16 = [xp[x].astype(_bf16) for x in it]
            xp = [_dot(x16[x], x16[x]) for x in it]
            r = [r[x] + xp[x] + _dot(r[x].astype(_bf16), xp[x].astype(_bf16)) for x in it]
            emit(2)
        sol = [rhs[x].astype(_f32) + _dot(r[x].astype(_bf16), rhs[x]) for x in it]
        u_hat = [sol[x][:, :HEAD_DIM] for x in it]
        w16 = [sol[x][:, HEAD_DIM:].astype(_bf16) for x in it]
        emit(2)
        hs = range(HEADS)
        st = [state_ref[h] for h in hs]
        for c in range(nc):
            xs = [c * HEADS + h for h in hs]
            rows = pl.ds(p0 + c * C, C)
            st16 = [st[h].astype(_bf16) for h in hs]
            u = [u_hat[xs[h]] - _dot(w16[xs[h]], st16[h]) for h in hs]
            u16 = [u[h].astype(_bf16) for h in hs]
            emit(1)
            o = [_dot(q_dec[xs[h]], st16[h]) + _dot(attn[xs[h]], u16[h]) for h in hs]
            emit(1)
            st = [st[h] * jnp.exp(glast[xs[h]]) + _dot_tn(k_dec[xs[h]], u16[h]) for h in hs]
            emit(1)
            for h in hs:
                on = o[h] * lax.rsqrt(jnp.mean(o[h] * o[h], axis=-1, keepdims=True) + EPS) * nw
                zg = z_ref[rows, hcol[h]].astype(_f32)
                o_ref[rows, hcol[h]] = (on * _silu(zg)).astype(o_ref.dtype)
        emit(3 * HEADS)
        for h in hs:
            state_ref[h] = st[h]
        return carry

    lax.fori_loop(0, npairs, pair_body, 0)


def _gdn(proj, g, norm_w, B, S, *, lc):
    T = proj.shape[0]
    nsteps = S // lc
    row = lambda b, s: b * nsteps + s
    return pl.pallas_call(
        functools.partial(_gdn_kernel, lc=lc),
        grid=(B, nsteps),
        in_specs=[
            pl.BlockSpec((lc, WIDTH), lambda b, s: (row(b, s), COL_GQ)),
            pl.BlockSpec((lc, WIDTH), lambda b, s: (row(b, s), COL_GK)),
            pl.BlockSpec((lc, WIDTH), lambda b, s: (row(b, s), COL_GV)),
            pl.BlockSpec((lc, WIDTH), lambda b, s: (row(b, s), COL_GZ)),
            pl.BlockSpec((lc, LANES), lambda b, s: (row(b, s), 0)),
            pl.BlockSpec((1, HEAD_DIM), lambda b, s: (0, 0)),
        ],
        out_specs=pl.BlockSpec((lc, WIDTH), lambda b, s: (row(b, s), 0)),
        out_shape=jax.ShapeDtypeStruct((T, WIDTH), _bf16),
        scratch_shapes=[
            pltpu.VMEM((2, LANES, 3 * WIDTH), _f32),
            pltpu.VMEM((HEADS, HEAD_DIM, HEAD_DIM), _f32),
        ],
        compiler_params=_params(("arbitrary", "arbitrary")),
        name="gdn",
    )(proj, proj, proj, proj, g, norm_w)


def _merge_kernel(x_ref, yf_ref, yg_ref, gf_ref, gg_ref, wf_ref, wg_ref, wo_ref, o_ref):
    a = _sigmoid(gf_ref[...].astype(_f32)) * _dot(yf_ref[...], wf_ref[...])
    b = _sigmoid(gg_ref[...].astype(_f32)) * _dot(yg_ref[...], wg_ref[...])
    y = (a + b).astype(_bf16)
    o_ref[...] = x_ref[...] + _dot(y, wo_ref[...])


def _merge(x2, y_fox, y_gdn, proj, wf, wg, wo, *, tm):
    T, D = x2.shape
    rows = lambda i: (i, 0)
    const = lambda i: (0, 0)
    return pl.pallas_call(
        _merge_kernel,
        grid=(T // tm,),
        in_specs=[
            pl.BlockSpec((tm, D), rows),
            pl.BlockSpec((tm, WIDTH), rows),
            pl.BlockSpec((tm, WIDTH), rows),
            pl.BlockSpec((tm, D), lambda i: (i, COL_GATE_FOX)),
            pl.BlockSpec((tm, D), lambda i: (i, COL_GATE_GDN)),
            pl.BlockSpec((WIDTH, D), const),
            pl.BlockSpec((WIDTH, D), const),
            pl.BlockSpec((D, D), const),
        ],
        out_specs=pl.BlockSpec((tm, D), rows),
        out_shape=jax.ShapeDtypeStruct((T, D), _f32),
        compiler_params=_params(("arbitrary",)),
        name="merge",
    )(x2, y_fox, y_gdn, proj, proj, wf, wg, wo)


def _ffn_kernel(h_ref, g_ref, wu_ref, cw_ref, wd_ref, gfin_ref, o_ref,
                ubuf_ref, tail_ref, act_ref, *, tf, final_norm):
    tm = h_ref.shape[0]
    d_ff = wd_ref.shape[0]
    step = pl.program_id(1)

    @pl.when(step == 0)
    def _():
        tail_ref[...] = jnp.zeros(tail_ref.shape, _f32)

    x = h_ref[...]
    hn = (x * lax.rsqrt(jnp.mean(x * x, axis=-1, keepdims=True) + EPS) * g_ref[...]).astype(_bf16)

    def conv(c0, half):
        up = _dot(hn, wu_ref[:, c0:c0 + tf])
        lo = half * tf
        ubuf_ref[0:SUBLANES, lo:lo + tf] = tail_ref[:, c0:c0 + tf]
        ubuf_ref[SUBLANES:, lo:lo + tf] = up
        tail_ref[:, c0:c0 + tf] = up[tm - SUBLANES:, :]
        acc = None
        for i in range(FFN_CONV):
            off = SUBLANES - (FFN_CONV - 1) + i
            term = ubuf_ref[off:off + tm, lo:lo + tf] * cw_ref[i:i + 1, c0:c0 + tf]
            acc = term if acc is None else acc + term
        return acc

    for c in range(d_ff // tf):
        gate = conv(c * tf, 0)
        val = conv(d_ff + c * tf, 1)
        act_ref[:, c * tf:(c + 1) * tf] = (_silu(gate) * val).astype(_bf16)

    h2 = x + _dot(act_ref[...], wd_ref[...])
    if final_norm:
        h2 = h2 * lax.rsqrt(jnp.mean(h2 * h2, axis=-1, keepdims=True) + EPS) * gfin_ref[...]
    o_ref[...] = h2


def _ffn(h1, g, wu, conv_w, wd, gfin, B, S, *, tm, tf, final_norm):
    T, D = h1.shape
    d_ff = wd.shape[0]
    nsteps = S // tm
    rows = lambda b, s: (b * nsteps + s, 0)
    const = lambda b, s: (0, 0)
    single = pl.Buffered(1)
    return pl.pallas_call(
        functools.partial(_ffn_kernel, tf=tf, final_norm=final_norm),
        grid=(B, nsteps),
        in_specs=[
            pl.BlockSpec((tm, D), rows),
            pl.BlockSpec((1, D), const),
            pl.BlockSpec((D, 2 * d_ff), const, pipeline_mode=single),
            pl.BlockSpec((FFN_CONV, 2 * d_ff), const),
            pl.BlockSpec((d_ff, D), const, pipeline_mode=single),
            pl.BlockSpec((1, D), const),
        ],
        out_specs=pl.BlockSpec((tm, D), rows),
        out_shape=jax.ShapeDtypeStruct((T, D), _f32),
        scratch_shapes=[
            pltpu.VMEM((tm + SUBLANES, 2 * tf), _f32),
            pltpu.VMEM((SUBLANES, 2 * d_ff), _f32),
            pltpu.VMEM((tm, d_ff), _bf16),
        ],
        compiler_params=_params(("arbitrary", "arbitrary")),
        name="ffn",
    )(h1, g, wu, conv_w, wd, gfin)


def _tiles(S):
    pick = lambda pref: next(t for t in pref if S % t == 0)
    return dict(
        tm_in=pick((1024, 512, 256, 128)),
        tq=pick((256, 128)),
        lc=pick((1024, 512, 256, 128)),
        tm_merge=pick((512, 256, 128)),
        tm_ffn=pick((512, 256, 128)),
    )


def kernel(x, norm_mix, w_in, fox_f_bias, gdn_conv_w, gdn_a_log, gdn_dt_bias, gdn_norm,
           w_branch_fox, w_branch_gdn, w_out, norm_ffn, w_up, ffn_conv_w, w_down, norm_final):
    B, S, D = x.shape
    L = norm_mix.shape[0]
    T = B * S
    assert D == WIDTH and S % LANES == 0
    t = _tiles(S)
    d_ff = w_down.shape[1]
    tf = 256 if d_ff % 256 == 0 else LANES
    h = x.reshape(T, D)
    o = 0
    seg = {}
    for name, width in (("fq", WIDTH), ("fk", WIDTH), ("fv", WIDTH), ("ff", HEADS),
                        ("gq", WIDTH), ("gk", WIDTH), ("gv", WIDTH), ("ga", HEADS), ("gb", HEADS),
                        ("gz", WIDTH), ("gate_fox", D), ("gate_gdn", D)):
        seg[name] = (o, o + width)
        o += width
    for l in range(L):
        w = w_in[l]
        cols = lambda n: w[:, seg[n][0]:seg[n][1]]
        w_big = jnp.concatenate(
            [cols("fq") * (HEAD_DIM ** -0.5 * LOG2E), cols("fk"), cols("fv"), cols("gq"), cols("gk"), cols("gv"),
             cols("gz"), cols("gate_fox"), cols("gate_gdn")], axis=1).astype(_bf16)
        w_small = jnp.concatenate(
            [cols("ff"), cols("ga"), cols("gb"), jnp.zeros((D, LANES - 3 * HEADS), w.dtype)], axis=1).astype(_bf16)
        zpad = jnp.zeros((LANES - 2 * HEADS,), _f32)
        gate_params = jnp.zeros((SUBLANES, LANES), _f32)
        gate_params = gate_params.at[0].set(jnp.concatenate([fox_f_bias[l], gdn_dt_bias[l], zpad]))
        gate_params = gate_params.at[1].set(jnp.concatenate([jnp.zeros((HEADS,), _f32), gdn_a_log[l], zpad]))

        proj, small = _in_proj(h, norm_mix[l][None, :], w_big, w_small, gdn_conv_w[l], S, tm=t["tm_in"])
        g, negc = _gate_prep(small, gate_params, B, S)
        y_fox = _fox(proj, negc, B, S, tq=t["tq"])
        y_gdn = _gdn(proj, g, gdn_norm[l][None, :], B, S, lc=t["lc"])
        h1 = _merge(h, y_fox, y_gdn, proj, w_branch_fox[l].astype(_bf16), w_branch_gdn[l].astype(_bf16),
                    w_out[l].astype(_bf16), tm=t["tm_merge"])
        h = _ffn(h1, norm_ffn[l][None, :], w_up[l].astype(_bf16), ffn_conv_w[l], w_down[l].astype(_bf16),
                 norm_final[None, :], B, S, tm=t["tm_ffn"], tf=tf, final_norm=(l == L - 1))
    if L == 0:
        raise ValueError("at least one layer expected")
    return h.reshape(B, S, D)
```

```python
import functools

import jax
import jax.numpy as jnp
from jax import lax
from jax.experimental import pallas as pl
from jax.experimental.pallas import tpu as pltpu

EPS = 1e-6
HEADS = 8
HEAD_DIM = 128
WIDTH = HEADS * HEAD_DIM
GDN_CONV = 4
GDN_CHUNK = 64
FFN_CONV = 3
LANES = 128
SUBLANES = 8
NEG_BIG = -1e30
LOG2E = 1.4426950408889634
FOX_HEADS_PER_STEP = 2

COL_FQ, COL_FK, COL_FV, COL_GQ, COL_GK, COL_GV, COL_GZ, COL_GATE_FOX, COL_GATE_GDN = range(9)
N_COLBLOCKS = 9
N_COLSTEPS = 3
COLSTEP_GDN = 1
IN_PROJ_SUB = 768
CONV_SUB = 256
LANE_FF, LANE_GA, LANE_GB = 0, HEADS, 2 * HEADS

VMEM_LIMIT = 56 * 1024 * 1024

_f32 = jnp.float32
_bf16 = jnp.bfloat16


def _dot(a, b):
    return jnp.dot(a, b, preferred_element_type=_f32)


def _dot_nt(a, b):
    return lax.dot_general(a, b, (((1,), (1,)), ((), ())), preferred_element_type=_f32)


def _dot_tn(a, b):
    return lax.dot_general(a, b, (((0,), (0,)), ((), ())), preferred_element_type=_f32)


def _sigmoid(z):
    return 1.0 / (1.0 + jnp.exp(-z))


def _silu(z):
    return z * _sigmoid(z)


def _causal_conv_rows(u, prev, w_ref, c0):
    taps = w_ref.shape[0]
    width = u.shape[1]
    first = lax.broadcasted_iota(jnp.int32, (SUBLANES, width), 0) == 0
    acc = acc_prev = None
    for i in range(taps):
        wi = w_ref[i:i + 1, c0:c0 + width]
        if i == 0:
            acc, acc_prev = u * wi, prev * wi
            continue
        rolled = pltpu.roll(acc, 1, axis=0)
        head = jnp.where(first, acc_prev[SUBLANES - 1:SUBLANES, :], rolled[0:SUBLANES, :])
        acc = u * wi + jnp.concatenate([head, rolled[SUBLANES:, :]], axis=0)
        acc_prev = prev * wi + pltpu.roll(acc_prev, 1, axis=0)
    return acc


def _params(sem, vmem=VMEM_LIMIT):
    return pltpu.CompilerParams(dimension_semantics=sem, vmem_limit_bytes=vmem)


def _in_proj_kernel(x_ref, g_ref, wb_ref, ws_ref, cw_ref, proj_ref, small_ref, hn_ref, ubuf_ref, tail_ref,
                    *, tiles_per_seq):
    i = pl.program_id(0)
    j = pl.program_id(1)
    tm = x_ref.shape[0]
    tn = wb_ref.shape[1]

    @pl.when(j == 0)
    def _():
        x = x_ref[...]
        ms = jnp.mean(x * x, axis=-1, keepdims=True)
        hn = (x * lax.rsqrt(ms + EPS) * g_ref[...]).astype(_bf16)
        hn_ref[...] = hn
        small_ref[...] = _dot(hn, ws_ref[...])

    @pl.when(j != COLSTEP_GDN)
    def _():
        hn = hn_ref[...]
        for c0 in range(0, tn, IN_PROJ_SUB):
            proj_ref[:, c0:c0 + IN_PROJ_SUB] = _dot(hn, wb_ref[:, c0:c0 + IN_PROJ_SUB]).astype(_bf16)

    @pl.when((j == COLSTEP_GDN) & (lax.rem(i, tiles_per_seq) == 0))
    def _():
        tail_ref[...] = jnp.zeros(tail_ref.shape, _f32)

    @pl.when(j == COLSTEP_GDN)
    def _():
        hn = hn_ref[...]
        w = CONV_SUB
        for c in range(tn // w):
            c0 = c * w
            lo = (c % 2) * w
            up = _dot(hn, wb_ref[:, c0:c0 + w])
            ubuf_ref[0:SUBLANES, lo:lo + w] = tail_ref[:, c0:c0 + w]
            ubuf_ref[SUBLANES:, lo:lo + w] = up
            tail_ref[:, c0:c0 + w] = up[tm - SUBLANES:, :]
            acc = None
            for t in range(GDN_CONV):
                off = SUBLANES - (GDN_CONV - 1) + t
                term = ubuf_ref[off:off + tm, lo:lo + w] * cw_ref[t:t + 1, c0:c0 + w]
                acc = term if acc is None else acc + term
            proj_ref[:, c0:c0 + w] = acc.astype(_bf16)


def _in_proj(x2, g, w_big, w_small, conv_w, S, *, tm):
    T, D = x2.shape
    N = w_big.shape[1]
    tn = N // N_COLSTEPS
    return pl.pallas_call(
        functools.partial(_in_proj_kernel, tiles_per_seq=S // tm),
        grid=(T // tm, N_COLSTEPS),
        in_specs=[
            pl.BlockSpec((tm, D), lambda i, j: (i, 0)),
            pl.BlockSpec((1, D), lambda i, j: (0, 0)),
            pl.BlockSpec((D, tn), lambda i, j: (0, j)),
            pl.BlockSpec((D, LANES), lambda i, j: (0, 0)),
            pl.BlockSpec((GDN_CONV, tn), lambda i, j: (0, 0)),
        ],
        out_specs=[
            pl.BlockSpec((tm, tn), lambda i, j: (i, j)),
            pl.BlockSpec((tm, LANES), lambda i, j: (i, 0)),
        ],
        out_shape=[
            jax.ShapeDtypeStruct((T, N), _bf16),
            jax.ShapeDtypeStruct((T, LANES), _f32),
        ],
        scratch_shapes=[
            pltpu.VMEM((tm, D), _bf16),
            pltpu.VMEM((tm + SUBLANES, 2 * CONV_SUB), _f32),
            pltpu.VMEM((SUBLANES, tn), _f32),
        ],
        compiler_params=_params(("arbitrary", "arbitrary")),
        name="in_proj",
    )(x2, g, w_big, w_small, conv_w)


def _split3(v):
    hi = v.astype(_bf16)
    r1 = v - hi.astype(_f32)
    mid = r1.astype(_bf16)
    lo = (r1 - mid.astype(_f32)).astype(_bf16)
    return hi, mid, lo


def _gate_prep_kernel(raw_ref, p_ref, g_ref, nb_ref):
    S = raw_ref.shape[0]
    nblk = S // LANES
    lane = lax.broadcasted_iota(jnp.int32, (LANES, LANES), 1)
    row = lax.broadcasted_iota(jnp.int32, (LANES, LANES), 0)
    tri = (row >= lane)
    tri_full = jnp.where(tri, 1.0, 0.0).astype(_bf16)
    same_chunk = (row >= GDN_CHUNK) == (lane >= GDN_CHUNK)
    tri_chunk = jnp.where(tri & same_chunk, 1.0, 0.0).astype(_bf16)
    bias = p_ref[0:1, :]
    neg_a = -jnp.exp(p_ref[1:2, :])
    is_f = lane < LANE_GA
    is_g = lane < LANE_GB
    carry = jnp.zeros((1, LANES), _f32)
    for r in range(nblk):
        z = raw_ref[r * LANES:(r + 1) * LANES, :] + bias
        t = jnp.log(1.0 + jnp.exp(-jnp.abs(z)))
        logsig = jnp.minimum(z, 0.0) - t
        softplus = jnp.maximum(z, 0.0) + t
        sig = _sigmoid(z)
        vals = jnp.where(is_f, logsig, jnp.where(is_g, neg_a * softplus, sig))
        hi, mid, lo = _split3(vals)
        cum_full = (_dot(tri_full, hi) + _dot(tri_full, mid)) + _dot(tri_full, lo) + carry
        cum_chunk = (_dot(tri_chunk, hi) + _dot(tri_chunk, mid)) + _dot(tri_chunk, lo)
        carry = cum_full[LANES - 1:LANES, :]
        out = jnp.where(is_f, cum_full, jnp.where(is_g, cum_chunk, vals))
        g_ref[r * LANES:(r + 1) * LANES, :] = out
        for h in range(HEADS):
            nb_ref[h, r * LANES:(r + 1) * LANES, :] = jnp.broadcast_to(
                out[:, LANE_FF + h:LANE_FF + h + 1] * (-LOG2E), (LANES, LANES))


def _gate_prep(small, gate_params, B, S):
    T = small.shape[0]
    return pl.pallas_call(
        _gate_prep_kernel,
        grid=(B,),
        in_specs=[
            pl.BlockSpec((S, LANES), lambda b: (b, 0)),
            pl.BlockSpec((SUBLANES, LANES), lambda b: (0, 0)),
        ],
        out_specs=[
            pl.BlockSpec((S, LANES), lambda b: (b, 0)),
            pl.BlockSpec((None, HEADS, S, LANES), lambda b: (b, 0, 0, 0)),
        ],
        out_shape=[
            jax.ShapeDtypeStruct((T, LANES), _f32),
            jax.ShapeDtypeStruct((B, HEADS, S, LANES), _f32),
        ],
        compiler_params=_params(("arbitrary",)),
        name="gate_prep",
    )(small, gate_params)


def _fox_kernel(q_ref, k_ref, v_ref, nb_ref, o_ref, vt_ref, *, tq):
    S = q_ref.shape[0]
    nq = S // tq
    hs = range(FOX_HEADS_PER_STEP)
    ri = lax.broadcasted_iota(jnp.int32, (tq, tq), 0)
    ci = lax.broadcasted_iota(jnp.int32, (tq, tq), 1)
    visible = ri <= ci
    eye = jnp.where(lax.broadcasted_iota(jnp.int32, (HEAD_DIM, HEAD_DIM), 0)
                    == lax.broadcasted_iota(jnp.int32, (HEAD_DIM, HEAD_DIM), 1), 1.0, 0.0).astype(_bf16)
    cols = lambda h: slice(h * HEAD_DIM, (h + 1) * HEAD_DIM)
    rows = lambda j: slice(j * tq, (j + 1) * tq)
    for j in range(nq):
        for h in hs:
            vt_ref[h, :, rows(j)] = _dot_nt(eye, v_ref[rows(j), cols(h)]).astype(_bf16)

    def score(i, j, h):
        s = _dot_nt(k_ref[rows(j), cols(h)], q_ref[rows(i), cols(h)])
        nb = nb_ref[h, rows(j), :]
        s = s + jnp.concatenate([nb] * (tq // LANES), axis=-1)
        if i == j:
            s = jnp.where(visible, s, NEG_BIG)
        return s

    tiles = [(i, j) for i in range(nq) for j in range(i + 1)]
    s_next = [score(*tiles[0], h) for h in hs]
    for t, (i, j) in enumerate(tiles):
        s = s_next
        if t + 1 < len(tiles):
            s_next = [score(*tiles[t + 1], h) for h in hs]
        if j == 0:
            m = [jnp.full((1, tq), NEG_BIG, _f32) for h in hs]
            l = [jnp.zeros((1, tq), _f32) for h in hs]
            acc = [jnp.zeros((HEAD_DIM, tq), _f32) for h in hs]
        m_new = [jnp.maximum(m[h], jnp.max(s[h], axis=0, keepdims=True)) for h in hs]
        alpha = [jnp.exp2(m[h] - m_new[h]) for h in hs]
        p = [jnp.exp2(s[h] - m_new[h]) for h in hs]
        l = [alpha[h] * l[h] + jnp.sum(p[h], axis=0, keepdims=True) for h in hs]
        acc = [alpha[h] * acc[h] + _dot(vt_ref[h, :, rows(j)], p[h].astype(_bf16)) for h in hs]
        m = m_new
        if j == i:
            for h in hs:
                o_ref[rows(i), cols(h)] = (acc[h] * (1.0 / l[h])).T.astype(o_ref.dtype)


def _fox(proj, negc, B, S, *, tq):
    T = proj.shape[0]
    hp = FOX_HEADS_PER_STEP
    width = hp * HEAD_DIM
    return pl.pallas_call(
        functools.partial(_fox_kernel, tq=tq),
        grid=(B, HEADS // hp),
        in_specs=[
            pl.BlockSpec((S, width), lambda b, h: (b, COL_FQ * (HEADS // hp) + h)),
            pl.BlockSpec((S, width), lambda b, h: (b, COL_FK * (HEADS // hp) + h)),
            pl.BlockSpec((S, width), lambda b, h: (b, COL_FV * (HEADS // hp) + h)),
            pl.BlockSpec((None, hp, S, LANES), lambda b, h: (b, h, 0, 0)),
        ],
        out_specs=pl.BlockSpec((S, width), lambda b, h: (b, h)),
        out_shape=jax.ShapeDtypeStruct((T, WIDTH), _bf16),
        scratch_shapes=[pltpu.VMEM((hp, HEAD_DIM, S), _bf16)],
        compiler_params=_params(("arbitrary", "arbitrary")),
        name="fox",
    )(proj, proj, proj, negc)


def _gdn_kernel(q_ref, k_ref, v_ref, z_ref, g_ref, nw_ref, o_ref, cbuf_ref, state_ref, *, lc):
    C = GDN_CHUNK
    P = LANES
    npairs = lc // P
    nc = P // C
    step = pl.program_id(1)
    srcs = (q_ref, k_ref, v_ref)

    @pl.when(step == 0)
    def _():
        state_ref[...] = jnp.zeros(state_ref.shape, _f32)

    ri = lax.broadcasted_iota(jnp.int32, (C, C), 0)
    ci = lax.broadcasted_iota(jnp.int32, (C, C), 1)
    incl = ci <= ri
    strict = ci < ri
    scale = HEAD_DIM ** -0.5
    nw = nw_ref[...]

    def act_unit(cb, p1, slot):
        c0 = (cb % HEADS) * HEAD_DIM
        act = _silu(srcs[cb // HEADS][pl.ds(p1, P), c0:c0 + HEAD_DIM].astype(_f32))
        if cb < 2 * HEADS:
            act = act * lax.rsqrt(jnp.sum(act * act, axis=-1, keepdims=True) + EPS)
        cbuf_ref[slot, :, cb * HEAD_DIM:(cb + 1) * HEAD_DIM] = act

    for cb in range(3 * HEADS):
        act_unit(cb, 0, 0)

    def pair_body(n, carry):
        p0 = pl.multiple_of(n * P, P)
        slot = lax.rem(n, 2)
        p1 = pl.multiple_of(jnp.minimum(n + 1, npairs - 1) * P, P)
        pending = iter(range(3 * HEADS))

        def emit(count):
            for _ in range(count):
                cb = next(pending, None)
                if cb is not None:
                    act_unit(cb, p1, 1 - slot)

        gpair = g_ref[pl.ds(p0, P), :]
        gpair_t = gpair.T
        items = [(c, h) for c in range(nc) for h in range(HEADS)]
        it = range(len(items))
        crow = [slice(c * C, (c + 1) * C) for c, h in items]
        hcol = [slice(h * HEAD_DIM, (h + 1) * HEAD_DIM) for c, h in items]
        q = [cbuf_ref[slot, crow[x], hcol[x]] for x in it]
        k = [cbuf_ref[slot, crow[x], WIDTH + items[x][1] * HEAD_DIM:WIDTH + (items[x][1] + 1) * HEAD_DIM]
             for x in it]
        v = [cbuf_ref[slot, crow[x], 2 * WIDTH + items[x][1] * HEAD_DIM:2 * WIDTH + (items[x][1] + 1) * HEAD_DIM]
             for x in it]
        gcol = [gpair[c * C:(c + 1) * C, LANE_GA + h:LANE_GA + h + 1] for c, h in items]
        bcol = [gpair[c * C:(c + 1) * C, LANE_GB + h:LANE_GB + h + 1] for c, h in items]
        grow = [gpair_t[LANE_GA + h:LANE_GA + h + 1, c * C:(c + 1) * C] for c, h in items]
        glast = [gcol[x][C - 1:C, :] for x in it]
        decay = [jnp.exp(jnp.where(incl, gcol[x] - grow[x], NEG_BIG)) for x in it]
        e_col = [jnp.exp(gcol[x]) for x in it]
        emit(2)
        kb = [k[x] * bcol[x] for x in it]
        k16 = [k[x].astype(_bf16) for x in it]
        a = [jnp.where(strict, _dot_nt(kb[x].astype(_bf16), k16[x]) * decay[x], 0.0) for x in it]
        attn = [(_dot_nt((q[x] * scale).astype(_bf16), k16[x]) * decay[x]).astype(_bf16) for x in it]
        emit(2)
        rhs = [jnp.concatenate([v[x] * bcol[x], kb[x] * e_col[x]], axis=-1).astype(_bf16) for x in it]
        q_dec = [(q[x] * (scale * e_col[x])).astype(_bf16) for x in it]
        k_dec = [(k[x] * jnp.exp(glast[x] - gcol[x])).astype(_bf16) for x in it]
        emit(2)
        xp = [-a[x] for x in it]
        r = xp
        for _ in range(5):
            x16 = [xp[x].astype(_bf16) for x in it]
            xp = [_dot(x16[x], x16[x]) for x in it]
            r = [r[x] + xp[x] + _dot(r[x].astype(_bf16), xp[x].astype(_bf16)) for x in it]
            emit(2)
        sol = [rhs[x].astype(_f32) + _dot(r[x].astype(_bf16), rhs[x]) for x in it]
        u_hat = [sol[x][:, :HEAD_DIM] for x in it]
        w16 = [sol[x][:, HEAD_DIM:].astype(_bf16) for x in it]
        emit(2)
        hs = range(HEADS)
        st = [state_ref[h] for h in hs]
        for c in range(nc):
            xs = [c * HEADS + h for h in hs]
            rows = pl.ds(p0 + c * C, C)
            st16 = [st[h].astype(_bf16) for h in hs]
            u = [u_hat[xs[h]] - _dot(w16[xs[h]], st16[h]) for h in hs]
            u16 = [u[h].astype(_bf16) for h in hs]
            emit(1)
            o = [_dot(q_dec[xs[h]], st16[h]) + _dot(attn[xs[h]], u16[h]) for h in hs]
            emit(1)
            st = [st[h] * jnp.exp(glast[xs[h]]) + _dot_tn(k_dec[xs[h]], u16[h]) for h in hs]
            emit(1)
            for h in hs:
                on = o[h] * lax.rsqrt(jnp.mean(o[h] * o[h], axis=-1, keepdims=True) + EPS) * nw
                zg = z_ref[rows, hcol[h]].astype(_f32)
                o_ref[rows, hcol[h]] = (on * _silu(zg)).astype(o_ref.dtype)
        emit(3 * HEADS)
        for h in hs:
            state_ref[h] = st[h]
        return carry

    lax.fori_loop(0, npairs, pair_body, 0)


def _gdn(proj, g, norm_w, B, S, *, lc):
    T = proj.shape[0]
    nsteps = S // lc
    row = lambda b, s: b * nsteps + s
    return pl.pallas_call(
        functools.partial(_gdn_kernel, lc=lc),
        grid=(B, nsteps),
        in_specs=[
            pl.BlockSpec((lc, WIDTH), lambda b, s: (row(b, s), COL_GQ)),
            pl.BlockSpec((lc, WIDTH), lambda b, s: (row(b, s), COL_GK)),
            pl.BlockSpec((lc, WIDTH), lambda b, s: (row(b, s), COL_GV)),
            pl.BlockSpec((lc, WIDTH), lambda b, s: (row(b, s), COL_GZ)),
            pl.BlockSpec((lc, LANES), lambda b, s: (row(b, s), 0)),
            pl.BlockSpec((1, HEAD_DIM), lambda b, s: (0, 0)),
        ],
        out_specs=pl.BlockSpec((lc, WIDTH), lambda b, s: (row(b, s), 0)),
        out_shape=jax.ShapeDtypeStruct((T, WIDTH), _bf16),
        scratch_shapes=[
            pltpu.VMEM((2, LANES, 3 * WIDTH), _f32),
            pltpu.VMEM((HEADS, HEAD_DIM, HEAD_DIM), _f32),
        ],
        compiler_params=_params(("arbitrary", "arbitrary")),
        name="gdn",
    )(proj, proj, proj, proj, g, norm_w)


def _merge_kernel(x_ref, yf_ref, yg_ref, gf_ref, gg_ref, wf_ref, wg_ref, wo_ref, o_ref):
    a = _sigmoid(gf_ref[...].astype(_f32)) * _dot(yf_ref[...], wf_ref[...])
    b = _sigmoid(gg_ref[...].astype(_f32)) * _dot(yg_ref[...], wg_ref[...])
    y = (a + b).astype(_bf16)
    o_ref[...] = x_ref[...] + _dot(y, wo_ref[...])


def _merge(x2, y_fox, y_gdn, proj, wf, wg, wo, *, tm):
    T, D = x2.shape
    rows = lambda i: (i, 0)
    const = lambda i: (0, 0)
    return pl.pallas_call(
        _merge_kernel,
        grid=(T // tm,),
        in_specs=[
            pl.BlockSpec((tm, D), rows),
            pl.BlockSpec((tm, WIDTH), rows),
            pl.BlockSpec((tm, WIDTH), rows),
            pl.BlockSpec((tm, D), lambda i: (i, COL_GATE_FOX)),
            pl.BlockSpec((tm, D), lambda i: (i, COL_GATE_GDN)),
            pl.BlockSpec((WIDTH, D), const),
            pl.BlockSpec((WIDTH, D), const),
            pl.BlockSpec((D, D), const),
        ],
        out_specs=pl.BlockSpec((tm, D), rows),
        out_shape=jax.ShapeDtypeStruct((T, D), _f32),
        compiler_params=_params(("arbitrary",)),
        name="merge",
    )(x2, y_fox, y_gdn, proj, proj, wf, wg, wo)


def _ffn_kernel(h_ref, g_ref, wu_ref, cw_ref, wd_ref, gfin_ref, o_ref,
                tail_ref, act_ref, *, tf, final_norm):
    tm = h_ref.shape[0]
    d_ff = wd_ref.shape[0]
    step = pl.program_id(1)

    @pl.when(step == 0)
    def _():
        tail_ref[...] = jnp.zeros(tail_ref.shape, _f32)

    x = h_ref[...]
    hn = (x * lax.rsqrt(jnp.mean(x * x, axis=-1, keepdims=True) + EPS) * g_ref[...]).astype(_bf16)

    def conv(c0):
        up = _dot(hn, wu_ref[:, c0:c0 + tf])
        prev = tail_ref[:, c0:c0 + tf]
        tail_ref[:, c0:c0 + tf] = up[tm - SUBLANES:, :]
        return _causal_conv_rows(up, prev, cw_ref, c0)

    for c in range(d_ff // tf):
        gate = conv(c * tf)
        val = conv(d_ff + c * tf)
        act_ref[:, c * tf:(c + 1) * tf] = (_silu(gate) * val).astype(_bf16)

    h2 = x + _dot(act_ref[...], wd_ref[...])
    if final_norm:
        h2 = h2 * lax.rsqrt(jnp.mean(h2 * h2, axis=-1, keepdims=True) + EPS) * gfin_ref[...]
    o_ref[...] = h2


def _ffn(h1, g, wu, conv_w, wd, gfin, B, S, *, tm, tf, final_norm):
    T, D = h1.shape
    d_ff = wd.shape[0]
    nsteps = S // tm
    rows = lambda b, s: (b * nsteps + s, 0)
    const = lambda b, s: (0, 0)
    single = pl.Buffered(1)
    return pl.pallas_call(
        functools.partial(_ffn_kernel, tf=tf, final_norm=final_norm),
        grid=(B, nsteps),
        in_specs=[
            pl.BlockSpec((tm, D), rows),
            pl.BlockSpec((1, D), const),
            pl.BlockSpec((D, 2 * d_ff), const, pipeline_mode=single),
            pl.BlockSpec((FFN_CONV, 2 * d_ff), const),
            pl.BlockSpec((d_ff, D), const, pipeline_mode=single),
            pl.BlockSpec((1, D), const),
        ],
        out_specs=pl.BlockSpec((tm, D), rows),
        out_shape=jax.ShapeDtypeStruct((T, D), _f32),
        scratch_shapes=[
            pltpu.VMEM((SUBLANES, 2 * d_ff), _f32),
            pltpu.VMEM((tm, d_ff), _bf16),
        ],
        compiler_params=_params(("arbitrary", "arbitrary")),
        name="ffn",
    )(h1, g, wu, conv_w, wd, gfin)


def _tiles(S):
    pick = lambda pref: next(t for t in pref if S % t == 0)
    return dict(
        tm_in=pick((1024, 512, 256, 128)),
        tq=pick((256, 128)),
        lc=pick((1024, 512, 256, 128)),
        tm_merge=pick((1024, 512, 256, 128)),
        tm_ffn=pick((1024, 512, 256, 128)),
    )


def kernel(x, norm_mix, w_in, fox_f_bias, gdn_conv_w, gdn_a_log, gdn_dt_bias, gdn_norm,
           w_branch_fox, w_branch_gdn, w_out, norm_ffn, w_up, ffn_conv_w, w_down, norm_final):
    B, S, D = x.shape
    L = norm_mix.shape[0]
    T = B * S
    assert D == WIDTH and S % LANES == 0
    t = _tiles(S)
    d_ff = w_down.shape[1]
    tf = 256 if d_ff % 256 == 0 else LANES
    h = x.reshape(T, D)
    o = 0
    seg = {}
    for name, width in (("fq", WIDTH), ("fk", WIDTH), ("fv", WIDTH), ("ff", HEADS),
                        ("gq", WIDTH), ("gk", WIDTH), ("gv", WIDTH), ("ga", HEADS), ("gb", HEADS),
                        ("gz", WIDTH), ("gate_fox", D), ("gate_gdn", D)):
        seg[name] = (o, o + width)
        o += width
    for l in range(L):
        w = w_in[l]
        cols = lambda n: w[:, seg[n][0]:seg[n][1]]
        w_big = jnp.concatenate(
            [cols("fq") * (HEAD_DIM ** -0.5 * LOG2E), cols("fk"), cols("fv"), cols("gq"), cols("gk"), cols("gv"),
             cols("gz"), cols("gate_fox"), cols("gate_gdn")], axis=1).astype(_bf16)
        w_small = jnp.concatenate(
            [cols("ff"), cols("ga"), cols("gb"), jnp.zeros((D, LANES - 3 * HEADS), w.dtype)], axis=1).astype(_bf16)
        zpad = jnp.zeros((LANES - 2 * HEADS,), _f32)
        gate_params = jnp.zeros((SUBLANES, LANES), _f32)
        gate_params = gate_params.at[0].set(jnp.concatenate([fox_f_bias[l], gdn_dt_bias[l], zpad]))
        gate_params = gate_params.at[1].set(jnp.concatenate([jnp.zeros((HEADS,), _f32), gdn_a_log[l], zpad]))

        proj, small = _in_proj(h, norm_mix[l][None, :], w_big, w_small, gdn_conv_w[l], S, tm=t["tm_in"])
        g, negc = _gate_prep(small, gate_params, B, S)
        y_fox = _fox(proj, negc, B, S, tq=t["tq"])
        y_gdn = _gdn(proj, g, gdn_norm[l][None, :], B, S, lc=t["lc"])
        h1 = _merge(h, y_fox, y_gdn, proj, w_branch_fox[l].astype(_bf16), w_branch_gdn[l].astype(_bf16),
                    w_out[l].astype(_bf16), tm=t["tm_merge"])
        h = _ffn(h1, norm_ffn[l][None, :], w_up[l].astype(_bf16), ffn_conv_w[l], w_down[l].astype(_bf16),
                 norm_final[None, :], B, S, tm=t["tm_ffn"], tf=tf, final_norm=(l == L - 1))
    if L == 0:
        raise ValueError("at least one layer expected")
    return h.reshape(B, S, D)
```

```python
import functools

import jax
import jax.numpy as jnp
from jax import lax
from jax.experimental import pallas as pl
from jax.experimental.pallas import tpu as pltpu

EPS = 1e-6
HEADS = 8
HEAD_DIM = 128
WIDTH = HEADS * HEAD_DIM
GDN_CONV = 4
GDN_CHUNK = 64
FFN_CONV = 3
LANES = 128
SUBLANES = 8
NEG_BIG = -1e30
LOG2E = 1.4426950408889634
FOX_HEADS_PER_STEP = 2
FOX_VT_ROWS = HEAD_DIM + 16

COL_FQ, COL_FK, COL_FV, COL_GQ, COL_GK, COL_GV, COL_GZ, COL_GATE_FOX, COL_GATE_GDN = range(9)
N_COLBLOCKS = 9
N_COLSTEPS = 3
COLSTEP_GDN = 1
IN_PROJ_SUB = 768
CONV_SUB = 256
LANE_FF, LANE_GA, LANE_GB = 0, HEADS, 2 * HEADS

VMEM_LIMIT = 56 * 1024 * 1024

_f32 = jnp.float32
_bf16 = jnp.bfloat16


def _dot(a, b):
    return jnp.dot(a, b, preferred_element_type=_f32)


def _dot_nt(a, b):
    return lax.dot_general(a, b, (((1,), (1,)), ((), ())), preferred_element_type=_f32)


def _dot_tn(a, b):
    return lax.dot_general(a, b, (((0,), (0,)), ((), ())), preferred_element_type=_f32)


def _sigmoid(z):
    return 1.0 / (1.0 + jnp.exp(-z))


def _silu(z):
    return z * _sigmoid(z)


def _causal_conv_rows(u, prev, w_ref, c0):
    taps = w_ref.shape[0]
    width = u.shape[1]
    first = lax.broadcasted_iota(jnp.int32, (SUBLANES, width), 0) == 0
    acc = acc_prev = None
    for i in range(taps):
        wi = w_ref[i:i + 1, c0:c0 + width]
        if i == 0:
            acc, acc_prev = u * wi, prev * wi
            continue
        rolled = pltpu.roll(acc, 1, axis=0)
        head = jnp.where(first, acc_prev[SUBLANES - 1:SUBLANES, :], rolled[0:SUBLANES, :])
        acc = u * wi + jnp.concatenate([head, rolled[SUBLANES:, :]], axis=0)
        acc_prev = prev * wi + pltpu.roll(acc_prev, 1, axis=0)
    return acc


def _params(sem, vmem=VMEM_LIMIT):
    return pltpu.CompilerParams(dimension_semantics=sem, vmem_limit_bytes=vmem)


def _in_proj_kernel(x_ref, g_ref, wb_ref, ws_ref, cw_ref, proj_ref, small_ref, hn_ref, ubuf_ref, tail_ref,
                    *, tiles_per_seq):
    i = pl.program_id(0)
    j = pl.program_id(1)
    tm = x_ref.shape[0]
    tn = wb_ref.shape[1]

    @pl.when(j == 0)
    def _():
        x = x_ref[...]
        ms = jnp.mean(x * x, axis=-1, keepdims=True)
        hn = (x * lax.rsqrt(ms + EPS) * g_ref[...]).astype(_bf16)
        hn_ref[...] = hn
        small_ref[...] = _dot(hn, ws_ref[...])

    @pl.when(j != COLSTEP_GDN)
    def _():
        hn = hn_ref[...]
        for c0 in range(0, tn, IN_PROJ_SUB):
            proj_ref[:, c0:c0 + IN_PROJ_SUB] = _dot(hn, wb_ref[:, c0:c0 + IN_PROJ_SUB]).astype(_bf16)

    @pl.when((j == COLSTEP_GDN) & (lax.rem(i, tiles_per_seq) == 0))
    def _():
        tail_ref[...] = jnp.zeros(tail_ref.shape, _f32)

    @pl.when(j == COLSTEP_GDN)
    def _():
        hn = hn_ref[...]
        w = CONV_SUB
        for c in range(tn // w):
            c0 = c * w
            lo = (c % 2) * w
            up = _dot(hn, wb_ref[:, c0:c0 + w])
            ubuf_ref[0:SUBLANES, lo:lo + w] = tail_ref[:, c0:c0 + w]
            ubuf_ref[SUBLANES:, lo:lo + w] = up
            tail_ref[:, c0:c0 + w] = up[tm - SUBLANES:, :]
            acc = None
            for t in range(GDN_CONV):
                off = SUBLANES - (GDN_CONV - 1) + t
                term = ubuf_ref[off:off + tm, lo:lo + w] * cw_ref[t:t + 1, c0:c0 + w]
                acc = term if acc is None else acc + term
            proj_ref[:, c0:c0 + w] = acc.astype(_bf16)


def _in_proj(x2, g, w_big, w_small, conv_w, S, *, tm):
    T, D = x2.shape
    N = w_big.shape[1]
    tn = N // N_COLSTEPS
    return pl.pallas_call(
        functools.partial(_in_proj_kernel, tiles_per_seq=S // tm),
        grid=(T // tm, N_COLSTEPS),
        in_specs=[
            pl.BlockSpec((tm, D), lambda i, j: (i, 0)),
            pl.BlockSpec((1, D), lambda i, j: (0, 0)),
            pl.BlockSpec((D, tn), lambda i, j: (0, j)),
            pl.BlockSpec((D, LANES), lambda i, j: (0, 0)),
            pl.BlockSpec((GDN_CONV, tn), lambda i, j: (0, 0)),
        ],
        out_specs=[
            pl.BlockSpec((tm, tn), lambda i, j: (i, j)),
            pl.BlockSpec((tm, LANES), lambda i, j: (i, 0)),
        ],
        out_shape=[
            jax.ShapeDtypeStruct((T, N), _bf16),
            jax.ShapeDtypeStruct((T, LANES), _f32),
        ],
        scratch_shapes=[
            pltpu.VMEM((tm, D), _bf16),
            pltpu.VMEM((tm + SUBLANES, 2 * CONV_SUB), _f32),
            pltpu.VMEM((SUBLANES, tn), _f32),
        ],
        compiler_params=_params(("arbitrary", "arbitrary")),
        name="in_proj",
    )(x2, g, w_big, w_small, conv_w)


def _split3(v):
    hi = v.astype(_bf16)
    r1 = v - hi.astype(_f32)
    mid = r1.astype(_bf16)
    lo = (r1 - mid.astype(_f32)).astype(_bf16)
    return hi, mid, lo


def _gate_prep_kernel(raw_ref, p_ref, g_ref, nb_ref):
    S = raw_ref.shape[0]
    nblk = S // LANES
    lane = lax.broadcasted_iota(jnp.int32, (LANES, LANES), 1)
    row = lax.broadcasted_iota(jnp.int32, (LANES, LANES), 0)
    tri = (row >= lane)
    tri_full = jnp.where(tri, 1.0, 0.0).astype(_bf16)
    same_chunk = (row >= GDN_CHUNK) == (lane >= GDN_CHUNK)
    tri_chunk = jnp.where(tri & same_chunk, 1.0, 0.0).astype(_bf16)
    bias = p_ref[0:1, :]
    neg_a = -jnp.exp(p_ref[1:2, :])
    is_f = lane < LANE_GA
    is_g = lane < LANE_GB
    carry = jnp.zeros((1, LANES), _f32)
    for r in range(nblk):
        z = raw_ref[r * LANES:(r + 1) * LANES, :] + bias
        t = jnp.log(1.0 + jnp.exp(-jnp.abs(z)))
        logsig = jnp.minimum(z, 0.0) - t
        softplus = jnp.maximum(z, 0.0) + t
        sig = _sigmoid(z)
        vals = jnp.where(is_f, logsig, jnp.where(is_g, neg_a * softplus, sig))
        hi, mid, lo = _split3(vals)
        cum_full = (_dot(tri_full, hi) + _dot(tri_full, mid)) + _dot(tri_full, lo) + carry
        cum_chunk = (_dot(tri_chunk, hi) + _dot(tri_chunk, mid)) + _dot(tri_chunk, lo)
        carry = cum_full[LANES - 1:LANES, :]
        out = jnp.where(is_f, cum_full, jnp.where(is_g, cum_chunk, vals))
        g_ref[r * LANES:(r + 1) * LANES, :] = out
        for h in range(HEADS):
            nb_ref[h, r * LANES:(r + 1) * LANES, :] = jnp.broadcast_to(
                out[:, LANE_FF + h:LANE_FF + h + 1] * (-LOG2E), (LANES, LANES))


def _gate_prep(small, gate_params, B, S):
    T = small.shape[0]
    return pl.pallas_call(
        _gate_prep_kernel,
        grid=(B,),
        in_specs=[
            pl.BlockSpec((S, LANES), lambda b: (b, 0)),
            pl.BlockSpec((SUBLANES, LANES), lambda b: (0, 0)),
        ],
        out_specs=[
            pl.BlockSpec((S, LANES), lambda b: (b, 0)),
            pl.BlockSpec((None, HEADS, S, LANES), lambda b: (b, 0, 0, 0)),
        ],
        out_shape=[
            jax.ShapeDtypeStruct((T, LANES), _f32),
            jax.ShapeDtypeStruct((B, HEADS, S, LANES), _f32),
        ],
        compiler_params=_params(("arbitrary",)),
        name="gate_prep",
    )(small, gate_params)


def _fox_kernel(q_ref, k_ref, v_ref, nb_ref, o_ref, vt_ref, *, tq):
    S = q_ref.shape[0]
    nq = S // tq
    hs = range(FOX_HEADS_PER_STEP)
    ri = lax.broadcasted_iota(jnp.int32, (tq, tq), 0)
    ci = lax.broadcasted_iota(jnp.int32, (tq, tq), 1)
    visible = ri <= ci
    eye = jnp.where(lax.broadcasted_iota(jnp.int32, (HEAD_DIM, HEAD_DIM), 0)
                    == lax.broadcasted_iota(jnp.int32, (HEAD_DIM, HEAD_DIM), 1), 1.0, 0.0).astype(_bf16)
    cols = lambda h: slice(h * HEAD_DIM, (h + 1) * HEAD_DIM)
    rows = lambda j: slice(j * tq, (j + 1) * tq)
    ones_row = jnp.where(lax.broadcasted_iota(jnp.int32, (FOX_VT_ROWS - HEAD_DIM, S), 0) == 0, 1.0, 0.0)
    for h in hs:
        vt_ref[h, HEAD_DIM:, :] = ones_row.astype(_bf16)
        for j in range(nq):
            vt_ref[h, 0:HEAD_DIM, rows(j)] = _dot_nt(eye, v_ref[rows(j), cols(h)]).astype(_bf16)

    def score(i, j, h):
        s = _dot_nt(k_ref[rows(j), cols(h)], q_ref[rows(i), cols(h)])
        nb = nb_ref[h, rows(j), :]
        s = s + jnp.concatenate([nb] * (tq // LANES), axis=-1)
        if i == j:
            s = jnp.where(visible, s, NEG_BIG)
        return s

    tiles = [(i, j) for i in range(nq) for j in range(i + 1)]
    s_next = [score(*tiles[0], h) for h in hs]
    for t, (i, j) in enumerate(tiles):
        s = s_next
        if t + 1 < len(tiles):
            s_next = [score(*tiles[t + 1], h) for h in hs]
        if j == 0:
            m = [jnp.full((1, tq), NEG_BIG, _f32) for h in hs]
            acc = [jnp.zeros((FOX_VT_ROWS, tq), _f32) for h in hs]
        m_new = [jnp.maximum(m[h], jnp.max(s[h], axis=0, keepdims=True)) for h in hs]
        alpha = [jnp.exp2(m[h] - m_new[h]) for h in hs]
        p = [jnp.exp2(s[h] - m_new[h]).astype(_bf16) for h in hs]
        acc = [alpha[h] * acc[h] + _dot(vt_ref[h, :, rows(j)], p[h]) for h in hs]
        m = m_new
        if j == i:
            for h in hs:
                inv_l = 1.0 / acc[h][HEAD_DIM:HEAD_DIM + 1, :]
                o_ref[rows(i), cols(h)] = (acc[h][0:HEAD_DIM, :] * inv_l).T.astype(o_ref.dtype)


def _fox(proj, negc, B, S, *, tq):
    T = proj.shape[0]
    hp = FOX_HEADS_PER_STEP
    width = hp * HEAD_DIM
    return pl.pallas_call(
        functools.partial(_fox_kernel, tq=tq),
        grid=(B, HEADS // hp),
        in_specs=[
            pl.BlockSpec((S, width), lambda b, h: (b, COL_FQ * (HEADS // hp) + h)),
            pl.BlockSpec((S, width), lambda b, h: (b, COL_FK * (HEADS // hp) + h)),
            pl.BlockSpec((S, width), lambda b, h: (b, COL_FV * (HEADS // hp) + h)),
            pl.BlockSpec((None, hp, S, LANES), lambda b, h: (b, h, 0, 0)),
        ],
        out_specs=pl.BlockSpec((S, width), lambda b, h: (b, h)),
        out_shape=jax.ShapeDtypeStruct((T, WIDTH), _bf16),
        scratch_shapes=[pltpu.VMEM((hp, FOX_VT_ROWS, S), _bf16)],
        compiler_params=_params(("arbitrary", "arbitrary")),
        name="fox",
    )(proj, proj, proj, negc)


def _gdn_kernel(q_ref, k_ref, v_ref, z_ref, g_ref, nw_ref, o_ref, cbuf_ref, state_ref, *, lc):
    C = GDN_CHUNK
    P = LANES
    npairs = lc // P
    nc = P // C
    step = pl.program_id(1)
    srcs = (q_ref, k_ref, v_ref)

    @pl.when(step == 0)
    def _():
        state_ref[...] = jnp.zeros(state_ref.shape, _f32)

    ri = lax.broadcasted_iota(jnp.int32, (C, C), 0)
    ci = lax.broadcasted_iota(jnp.int32, (C, C), 1)
    incl = ci <= ri
    strict = ci < ri
    diag = ci == ri
    scale = HEAD_DIM ** -0.5
    nw = nw_ref[...]

    def act_unit(cb, p1, slot):
        c0 = (cb % HEADS) * HEAD_DIM
        act = _silu(srcs[cb // HEADS][pl.ds(p1, P), c0:c0 + HEAD_DIM].astype(_f32))
        if cb < 2 * HEADS:
            act = act * lax.rsqrt(jnp.sum(act * act, axis=-1, keepdims=True) + EPS)
        cbuf_ref[slot, :, cb * HEAD_DIM:(cb + 1) * HEAD_DIM] = act

    for cb in range(3 * HEADS):
        act_unit(cb, 0, 0)

    def pair_body(n, carry):
        p0 = pl.multiple_of(n * P, P)
        slot = lax.rem(n, 2)
        p1 = pl.multiple_of(jnp.minimum(n + 1, npairs - 1) * P, P)
        pending = iter(range(3 * HEADS))

        def emit(count):
            for _ in range(count):
                cb = next(pending, None)
                if cb is not None:
                    act_unit(cb, p1, 1 - slot)

        gpair = g_ref[pl.ds(p0, P), :]
        gpair_t = gpair.T
        items = [(c, h) for c in range(nc) for h in range(HEADS)]
        it = range(len(items))
        crow = [slice(c * C, (c + 1) * C) for c, h in items]
        hcol = [slice(h * HEAD_DIM, (h + 1) * HEAD_DIM) for c, h in items]
        q = [cbuf_ref[slot, crow[x], hcol[x]] for x in it]
        k = [cbuf_ref[slot, crow[x], WIDTH + items[x][1] * HEAD_DIM:WIDTH + (items[x][1] + 1) * HEAD_DIM]
             for x in it]
        v = [cbuf_ref[slot, crow[x], 2 * WIDTH + items[x][1] * HEAD_DIM:2 * WIDTH + (items[x][1] + 1) * HEAD_DIM]
             for x in it]
        gcol = [gpair[c * C:(c + 1) * C, LANE_GA + h:LANE_GA + h + 1] for c, h in items]
        bcol = [gpair[c * C:(c + 1) * C, LANE_GB + h:LANE_GB + h + 1] for c, h in items]
        grow = [gpair_t[LANE_GA + h:LANE_GA + h + 1, c * C:(c + 1) * C] for c, h in items]
        glast = [gcol[x][C - 1:C, :] for x in it]
        decay = [jnp.exp(jnp.where(incl, gcol[x] - grow[x], NEG_BIG)) for x in it]
        e_col = [jnp.exp(gcol[x]) for x in it]
        emit(2)
        kb = [k[x] * bcol[x] for x in it]
        k16 = [k[x].astype(_bf16) for x in it]
        a = [jnp.where(strict, _dot_nt(kb[x].astype(_bf16), k16[x]) * decay[x], 0.0) for x in it]
        attn = [(_dot_nt((q[x] * scale).astype(_bf16), k16[x]) * decay[x]).astype(_bf16) for x in it]
        emit(2)
        rhs = [jnp.concatenate([v[x] * bcol[x], kb[x] * e_col[x]], axis=-1).astype(_bf16) for x in it]
        q_dec = [(q[x] * (scale * e_col[x])).astype(_bf16) for x in it]
        k_dec = [(k[x] * jnp.exp(glast[x] - gcol[x])).astype(_bf16) for x in it]
        emit(2)
        xp = [-a[x] for x in it]
        tinv = [jnp.where(diag, 1.0, xp[x]) for x in it]
        for level in range(5):
            x16 = [xp[x].astype(_bf16) for x in it]
            xp = [_dot(x16[x], x16[x]) for x in it]
            if level > 0:
                tinv = [tinv[x] + _dot(tinv[x].astype(_bf16), x16[x]) for x in it]
            emit(2)
        tinv = [tinv[x] + _dot(tinv[x].astype(_bf16), xp[x].astype(_bf16)) for x in it]
        sol = [_dot(tinv[x].astype(_bf16), rhs[x]) for x in it]
        u_hat = [sol[x][:, :HEAD_DIM] for x in it]
        w16 = [sol[x][:, HEAD_DIM:].astype(_bf16) for x in it]
        emit(2)
        hs = range(HEADS)
        st = [state_ref[h] for h in hs]
        for c in range(nc):
            xs = [c * HEADS + h for h in hs]
            rows = pl.ds(p0 + c * C, C)
            st16 = [st[h].astype(_bf16) for h in hs]
            u = [u_hat[xs[h]] - _dot(w16[xs[h]], st16[h]) for h in hs]
            u16 = [u[h].astype(_bf16) for h in hs]
            emit(1)
            o = [_dot(q_dec[xs[h]], st16[h]) + _dot(attn[xs[h]], u16[h]) for h in hs]
            emit(1)
            st = [st[h] * jnp.exp(glast[xs[h]]) + _dot_tn(k_dec[xs[h]], u16[h]) for h in hs]
            emit(1)
            for h in hs:
                on = o[h] * lax.rsqrt(jnp.mean(o[h] * o[h], axis=-1, keepdims=True) + EPS) * nw
                zg = z_ref[rows, hcol[h]].astype(_f32)
                o_ref[rows, hcol[h]] = (on * _silu(zg)).astype(o_ref.dtype)
        emit(3 * HEADS)
        for h in hs:
            state_ref[h] = st[h]
        return carry

    lax.fori_loop(0, npairs, pair_body, 0)


def _gdn(proj, g, norm_w, B, S, *, lc):
    T = proj.shape[0]
    nsteps = S // lc
    row = lambda b, s: b * nsteps + s
    return pl.pallas_call(
        functools.partial(_gdn_kernel, lc=lc),
        grid=(B, nsteps),
        in_specs=[
            pl.BlockSpec((lc, WIDTH), lambda b, s: (row(b, s), COL_GQ)),
            pl.BlockSpec((lc, WIDTH), lambda b, s: (row(b, s), COL_GK)),
            pl.BlockSpec((lc, WIDTH), lambda b, s: (row(b, s), COL_GV)),
            pl.BlockSpec((lc, WIDTH), lambda b, s: (row(b, s), COL_GZ)),
            pl.BlockSpec((lc, LANES), lambda b, s: (row(b, s), 0)),
            pl.BlockSpec((1, HEAD_DIM), lambda b, s: (0, 0)),
        ],
        out_specs=pl.BlockSpec((lc, WIDTH), lambda b, s: (row(b, s), 0)),
        out_shape=jax.ShapeDtypeStruct((T, WIDTH), _bf16),
        scratch_shapes=[
            pltpu.VMEM((2, LANES, 3 * WIDTH), _f32),
            pltpu.VMEM((HEADS, HEAD_DIM, HEAD_DIM), _f32),
        ],
        compiler_params=_params(("arbitrary", "arbitrary")),
        name="gdn",
    )(proj, proj, proj, proj, g, norm_w)


def _merge_kernel(x_ref, yf_ref, yg_ref, gf_ref, gg_ref, wf_ref, wg_ref, wo_ref, o_ref):
    a = _sigmoid(gf_ref[...].astype(_f32)) * _dot(yf_ref[...], wf_ref[...])
    b = _sigmoid(gg_ref[...].astype(_f32)) * _dot(yg_ref[...], wg_ref[...])
    y = (a + b).astype(_bf16)
    o_ref[...] = x_ref[...] + _dot(y, wo_ref[...])


def _merge(x2, y_fox, y_gdn, proj, wf, wg, wo, *, tm):
    T, D = x2.shape
    rows = lambda i: (i, 0)
    const = lambda i: (0, 0)
    return pl.pallas_call(
        _merge_kernel,
        grid=(T // tm,),
        in_specs=[
            pl.BlockSpec((tm, D), rows),
            pl.BlockSpec((tm, WIDTH), rows),
            pl.BlockSpec((tm, WIDTH), rows),
            pl.BlockSpec((tm, D), lambda i: (i, COL_GATE_FOX)),
            pl.BlockSpec((tm, D), lambda i: (i, COL_GATE_GDN)),
            pl.BlockSpec((WIDTH, D), const),
            pl.BlockSpec((WIDTH, D), const),
            pl.BlockSpec((D, D), const),
        ],
        out_specs=pl.BlockSpec((tm, D), rows),
        out_shape=jax.ShapeDtypeStruct((T, D), _f32),
        compiler_params=_params(("arbitrary",)),
        name="merge",
    )(x2, y_fox, y_gdn, proj, proj, wf, wg, wo)


def _ffn_kernel(h_ref, g_ref, wu_ref, cw_ref, wd_ref, gfin_ref, o_ref,
                tail_ref, act_ref, *, tf, final_norm):
    tm = h_ref.shape[0]
    d_ff = wd_ref.shape[0]
    step = pl.program_id(1)

    @pl.when(step == 0)
    def _():
        tail_ref[...] = jnp.zeros(tail_ref.shape, _f32)

    x = h_ref[...]
    hn = (x * lax.rsqrt(jnp.mean(x * x, axis=-1, keepdims=True) + EPS) * g_ref[...]).astype(_bf16)

    def conv(c0):
        up = _dot(hn, wu_ref[:, c0:c0 + tf])
        prev = tail_ref[:, c0:c0 + tf]
        tail_ref[:, c0:c0 + tf] = up[tm - SUBLANES:, :]
        return _causal_conv_rows(up, prev, cw_ref, c0)

    for c in range(d_ff // tf):
        gate = conv(c * tf)
        val = conv(d_ff + c * tf)
        act_ref[:, c * tf:(c + 1) * tf] = (_silu(gate) * val).astype(_bf16)

    h2 = x + _dot(act_ref[...], wd_ref[...])
    if final_norm:
        h2 = h2 * lax.rsqrt(jnp.mean(h2 * h2, axis=-1, keepdims=True) + EPS) * gfin_ref[...]
    o_ref[...] = h2


def _ffn(h1, g, wu, conv_w, wd, gfin, B, S, *, tm, tf, final_norm):
    T, D = h1.shape
    d_ff = wd.shape[0]
    nsteps = S // tm
    rows = lambda b, s: (b * nsteps + s, 0)
    const = lambda b, s: (0, 0)
    single = pl.Buffered(1)
    return pl.pallas_call(
        functools.partial(_ffn_kernel, tf=tf, final_norm=final_norm),
        grid=(B, nsteps),
        in_specs=[
            pl.BlockSpec((tm, D), rows),
            pl.BlockSpec((1, D), const),
            pl.BlockSpec((D, 2 * d_ff), const, pipeline_mode=single),
            pl.BlockSpec((FFN_CONV, 2 * d_ff), const),
            pl.BlockSpec((d_ff, D), const, pipeline_mode=single),
            pl.BlockSpec((1, D), const),
        ],
        out_specs=pl.BlockSpec((tm, D), rows),
        out_shape=jax.ShapeDtypeStruct((T, D), _f32),
        scratch_shapes=[
            pltpu.VMEM((SUBLANES, 2 * d_ff), _f32),
            pltpu.VMEM((tm, d_ff), _bf16),
        ],
        compiler_params=_params(("arbitrary", "arbitrary")),
        name="ffn",
    )(h1, g, wu, conv_w, wd, gfin)


def _tiles(S):
    pick = lambda pref: next(t for t in pref if S % t == 0)
    return dict(
        tm_in=pick((1024, 512, 256, 128)),
        tq=pick((256, 128)),
        lc=pick((1024, 512, 256, 128)),
        tm_merge=pick((1024, 512, 256, 128)),
        tm_ffn=pick((1024, 512, 256, 128)),
    )


def kernel(x, norm_mix, w_in, fox_f_bias, gdn_conv_w, gdn_a_log, gdn_dt_bias, gdn_norm,
           w_branch_fox, w_branch_gdn, w_out, norm_ffn, w_up, ffn_conv_w, w_down, norm_final):
    B, S, D = x.shape
    L = norm_mix.shape[0]
    T = B * S
    assert D == WIDTH and S % LANES == 0
    t = _tiles(S)
    d_ff = w_down.shape[1]
    tf = 256 if d_ff % 256 == 0 else LANES
    h = x.reshape(T, D)
    o = 0
    seg = {}
    for name, width in (("fq", WIDTH), ("fk", WIDTH), ("fv", WIDTH), ("ff", HEADS),
                        ("gq", WIDTH), ("gk", WIDTH), ("gv", WIDTH), ("ga", HEADS), ("gb", HEADS),
                        ("gz", WIDTH), ("gate_fox", D), ("gate_gdn", D)):
        seg[name] = (o, o + width)
        o += width
    for l in range(L):
        w = w_in[l]
        cols = lambda n: w[:, seg[n][0]:seg[n][1]]
        w_big = jnp.concatenate(
            [cols("fq") * (HEAD_DIM ** -0.5 * LOG2E), cols("fk"), cols("fv"), cols("gq"), cols("gk"), cols("gv"),
             cols("gz"), cols("gate_fox"), cols("gate_gdn")], axis=1).astype(_bf16)
        w_small = jnp.concatenate(
            [cols("ff"), cols("ga"), cols("gb"), jnp.zeros((D, LANES - 3 * HEADS), w.dtype)], axis=1).astype(_bf16)
        zpad = jnp.zeros((LANES - 2 * HEADS,), _f32)
        gate_params = jnp.zeros((SUBLANES, LANES), _f32)
        gate_params = gate_params.at[0].set(jnp.concatenate([fox_f_bias[l], gdn_dt_bias[l], zpad]))
        gate_params = gate_params.at[1].set(jnp.concatenate([jnp.zeros((HEADS,), _f32), gdn_a_log[l], zpad]))

        proj, small = _in_proj(h, norm_mix[l][None, :], w_big, w_small, gdn_conv_w[l], S, tm=t["tm_in"])
        g, negc = _gate_prep(small, gate_params, B, S)
        y_fox = _fox(proj, negc, B, S, tq=t["tq"])
        y_gdn = _gdn(proj, g, gdn_norm[l][None, :], B, S, lc=t["lc"])
        h1 = _merge(h, y_fox, y_gdn, proj, w_branch_fox[l].astype(_bf16), w_branch_gdn[l].astype(_bf16),
                    w_out[l].astype(_bf16), tm=t["tm_merge"])
        h = _ffn(h1, norm_ffn[l][None, :], w_up[l].astype(_bf16), ffn_conv_w[l], w_down[l].astype(_bf16),
                 norm_final[None, :], B, S, tm=t["tm_ffn"], tf=tf, final_norm=(l == L - 1))
    if L == 0:
        raise ValueError("at least one layer expected")
    return h.reshape(B, S, D)
```

```python
import functools

import jax
import jax.numpy as jnp
from jax import lax
from jax.experimental import pallas as pl
from jax.experimental.pallas import tpu as pltpu

EPS = 1e-6
HEADS = 8
HEAD_DIM = 128
WIDTH = HEADS * HEAD_DIM
GDN_CONV = 4
GDN_CHUNK = 64
GDN_GROUP = 128
FFN_CONV = 3
LANES = 128
SUBLANES = 8
NEG_BIG = -1e30
LOG2E = 1.4426950408889634
FOX_HEADS_PER_STEP = 2
FOX_WIDE_KEYS = 256
FOX_VT_ROWS = HEAD_DIM + 16

COL_FQ, COL_FK, COL_FV, COL_GQ, COL_GK, COL_GV, COL_GZ, COL_GATE_FOX, COL_GATE_GDN = range(9)
N_COLBLOCKS = 9
N_COLSTEPS = 3
COLSTEP_GDN = 1
COLSTEP_GATES = 2
IN_PROJ_SUB = 768
CONV_SUB = 256
LANE_FF, LANE_GA, LANE_GB = 0, HEADS, 2 * HEADS

VMEM_LIMIT = 56 * 1024 * 1024

_f32 = jnp.float32
_bf16 = jnp.bfloat16


def _dot(a, b):
    return jnp.dot(a, b, preferred_element_type=_f32)


def _dot_nt(a, b):
    return lax.dot_general(a, b, (((1,), (1,)), ((), ())), preferred_element_type=_f32)


def _dot_tn(a, b):
    return lax.dot_general(a, b, (((0,), (0,)), ((), ())), preferred_element_type=_f32)


def _sigmoid(z):
    return 1.0 / (1.0 + jnp.exp(-z))


def _silu(z):
    return z * _sigmoid(z)


def _causal_conv_rows(u, prev, w_ref, c0):
    taps = w_ref.shape[0]
    width = u.shape[1]
    first = lax.broadcasted_iota(jnp.int32, (SUBLANES, width), 0) == 0
    acc = acc_prev = None
    for i in range(taps):
        wi = w_ref[i:i + 1, c0:c0 + width]
        if i == 0:
            acc, acc_prev = u * wi, prev * wi
            continue
        rolled = pltpu.roll(acc, 1, axis=0)
        head = jnp.where(first, acc_prev[SUBLANES - 1:SUBLANES, :], rolled[0:SUBLANES, :])
        acc = u * wi + jnp.concatenate([head, rolled[SUBLANES:, :]], axis=0)
        acc_prev = prev * wi + pltpu.roll(acc_prev, 1, axis=0)
    return acc


def _params(sem, vmem=VMEM_LIMIT):
    return pltpu.CompilerParams(dimension_semantics=sem, vmem_limit_bytes=vmem)


def _split3(v):
    hi = v.astype(_bf16)
    r1 = v - hi.astype(_f32)
    mid = r1.astype(_bf16)
    lo = (r1 - mid.astype(_f32)).astype(_bf16)
    return hi, mid, lo


def _in_proj_kernel(x_ref, g_ref, wb_ref, ws_ref, cw_ref, p_ref, proj_ref, gate_ref, nb_ref,
                    hn_ref, small_ref, ubuf_ref, tail_ref, carry_ref, *, tiles_per_seq):
    i = pl.program_id(0)
    j = pl.program_id(1)
    tm = x_ref.shape[0]
    tn = wb_ref.shape[1]
    first_tile = lax.rem(i, tiles_per_seq) == 0

    @pl.when(j == 0)
    def _():
        x = x_ref[...]
        ms = jnp.mean(x * x, axis=-1, keepdims=True)
        hn = (x * lax.rsqrt(ms + EPS) * g_ref[...]).astype(_bf16)
        hn_ref[...] = hn
        small_ref[...] = _dot(hn, ws_ref[...])

    def plain_step(between=None):
        hn = hn_ref[...]
        for n, c0 in enumerate(range(0, tn, IN_PROJ_SUB)):
            proj_ref[:, c0:c0 + IN_PROJ_SUB] = _dot(hn, wb_ref[:, c0:c0 + IN_PROJ_SUB]).astype(_bf16)
            if between is not None:
                between(n)

    @pl.when(j == 0)
    def _():
        plain_step()

    @pl.when((j == COLSTEP_GDN) & first_tile)
    def _():
        tail_ref[...] = jnp.zeros(tail_ref.shape, _f32)

    @pl.when(j == COLSTEP_GDN)
    def _():
        hn = hn_ref[...]
        w = CONV_SUB
        for c in range(tn // w):
            c0 = c * w
            lo = (c % 2) * w
            up = _dot(hn, wb_ref[:, c0:c0 + w])
            ubuf_ref[0:SUBLANES, lo:lo + w] = tail_ref[:, c0:c0 + w]
            ubuf_ref[SUBLANES:, lo:lo + w] = up
            tail_ref[:, c0:c0 + w] = up[tm - SUBLANES:, :]
            acc = None
            for t in range(GDN_CONV):
                off = SUBLANES - (GDN_CONV - 1) + t
                term = ubuf_ref[off:off + tm, lo:lo + w] * cw_ref[t:t + 1, c0:c0 + w]
                acc = term if acc is None else acc + term
            proj_ref[:, c0:c0 + w] = acc.astype(_bf16)

    @pl.when((j == COLSTEP_GATES) & first_tile)
    def _():
        carry_ref[...] = jnp.zeros(carry_ref.shape, _f32)

    @pl.when(j == COLSTEP_GATES)
    def _():
        lane = lax.broadcasted_iota(jnp.int32, (LANES, LANES), 1)
        row = lax.broadcasted_iota(jnp.int32, (LANES, LANES), 0)
        tri = (row >= lane)
        tri_full = jnp.where(tri, 1.0, 0.0).astype(_bf16)
        same_chunk = (row >= GDN_CHUNK) == (lane >= GDN_CHUNK)
        tri_chunk = jnp.where(tri & same_chunk, 1.0, 0.0).astype(_bf16)
        bias = p_ref[0:1, :]
        neg_a = -jnp.exp(p_ref[1:2, :])
        is_f = lane < LANE_GA
        is_g = lane < LANE_GB

        def gate_block(r):
            rows = slice(r * LANES, (r + 1) * LANES)
            z = small_ref[rows, :] + bias
            t = jnp.log(1.0 + jnp.exp(-jnp.abs(z)))
            logsig = jnp.minimum(z, 0.0) - t
            softplus = jnp.maximum(z, 0.0) + t
            vals = jnp.where(is_f, logsig, jnp.where(is_g, neg_a * softplus, _sigmoid(z)))
            hi, mid, lo = _split3(vals)
            cum_full = (_dot(tri_full, hi) + _dot(tri_full, mid)) + _dot(tri_full, lo) + carry_ref[0:1, :]
            cum_chunk = (_dot(tri_chunk, hi) + _dot(tri_chunk, mid)) + _dot(tri_chunk, lo)
            carry_ref[...] = jnp.broadcast_to(cum_full[LANES - 1:LANES, :], carry_ref.shape)
            out = jnp.where(is_f, cum_full, jnp.where(is_g, cum_chunk, vals))
            gate_ref[rows, :] = out
            for h in range(HEADS):
                nb_ref[h, rows, :] = jnp.broadcast_to(out[:, LANE_FF + h:LANE_FF + h + 1] * (-LOG2E),
                                                      (LANES, LANES))

        nblk = tm // LANES
        nsub = tn // IN_PROJ_SUB
        per = -(-nblk // nsub)

        def between(n):
            for r in range(n * per, min((n + 1) * per, nblk)):
                gate_block(r)

        plain_step(between)


def _in_proj(x2, g, w_big, w_small, conv_w, gate_params, B, S, *, tm):
    T, D = x2.shape
    N = w_big.shape[1]
    tn = N // N_COLSTEPS
    tps = S // tm
    return pl.pallas_call(
        functools.partial(_in_proj_kernel, tiles_per_seq=tps),
        grid=(T // tm, N_COLSTEPS),
        in_specs=[
            pl.BlockSpec((tm, D), lambda i, j: (i, 0)),
            pl.BlockSpec((1, D), lambda i, j: (0, 0)),
            pl.BlockSpec((D, tn), lambda i, j: (0, j)),
            pl.BlockSpec((D, LANES), lambda i, j: (0, 0)),
            pl.BlockSpec((GDN_CONV, tn), lambda i, j: (0, 0)),
            pl.BlockSpec((SUBLANES, LANES), lambda i, j: (0, 0)),
        ],
        out_specs=[
            pl.BlockSpec((tm, tn), lambda i, j: (i, j)),
            pl.BlockSpec((tm, LANES), lambda i, j: (i, 0)),
            pl.BlockSpec((None, HEADS, tm, LANES), lambda i, j: (i // tps, 0, i % tps, 0)),
        ],
        out_shape=[
            jax.ShapeDtypeStruct((T, N), _bf16),
            jax.ShapeDtypeStruct((T, LANES), _f32),
            jax.ShapeDtypeStruct((B, HEADS, S, LANES), _f32),
        ],
        scratch_shapes=[
            pltpu.VMEM((tm, D), _bf16),
            pltpu.VMEM((tm, LANES), _f32),
            pltpu.VMEM((tm + SUBLANES, 2 * CONV_SUB), _f32),
            pltpu.VMEM((SUBLANES, tn), _f32),
            pltpu.VMEM((SUBLANES, LANES), _f32),
        ],
        compiler_params=_params(("arbitrary", "arbitrary")),
        name="in_proj",
    )(x2, g, w_big, w_small, conv_w, gate_params)


def _fox_kernel(q_ref, k_ref, v_ref, nb_ref, o_ref, vt_ref, *, tq):
    S = q_ref.shape[0]
    nq = S // tq
    hs = range(FOX_HEADS_PER_STEP)
    ri = lax.broadcasted_iota(jnp.int32, (tq, tq), 0)
    ci = lax.broadcasted_iota(jnp.int32, (tq, tq), 1)
    visible = ri <= ci
    eye = jnp.where(lax.broadcasted_iota(jnp.int32, (HEAD_DIM, HEAD_DIM), 0)
                    == lax.broadcasted_iota(jnp.int32, (HEAD_DIM, HEAD_DIM), 1), 1.0, 0.0).astype(_bf16)
    cols = lambda h: slice(h * HEAD_DIM, (h + 1) * HEAD_DIM)
    rows = lambda j: slice(j * tq, (j + 1) * tq)
    ones_row = jnp.where(lax.broadcasted_iota(jnp.int32, (FOX_VT_ROWS - HEAD_DIM, S), 0) == 0, 1.0, 0.0)
    for h in hs:
        vt_ref[h, HEAD_DIM:, :] = ones_row.astype(_bf16)
        for j in range(nq):
            vt_ref[h, 0:HEAD_DIM, rows(j)] = _dot_nt(eye, v_ref[rows(j), cols(h)]).astype(_bf16)

    def score(i, k0, nk, h):
        keys = slice(k0, k0 + nk)
        s = _dot_nt(k_ref[keys, cols(h)], q_ref[rows(i), cols(h)])
        nb = nb_ref[h, keys, :]
        s = s + jnp.concatenate([nb] * (tq // LANES), axis=-1)
        if k0 == i * tq:
            s = jnp.where(visible, s, NEG_BIG)
        return s

    tiles = []
    for i in range(nq):
        k0 = 0
        while k0 < i * tq:
            nk = min(FOX_WIDE_KEYS, i * tq - k0)
            tiles.append((i, k0, nk))
            k0 += nk
        tiles.append((i, i * tq, tq))

    s_next = [score(*tiles[0], h) for h in hs]
    for t, (i, k0, nk) in enumerate(tiles):
        s = s_next
        if t + 1 < len(tiles):
            s_next = [score(*tiles[t + 1], h) for h in hs]
        if k0 == 0:
            m = [jnp.full((1, tq), NEG_BIG, _f32) for h in hs]
            acc = [jnp.zeros((FOX_VT_ROWS, tq), _f32) for h in hs]
        m_new = [jnp.maximum(m[h], jnp.max(s[h], axis=0, keepdims=True)) for h in hs]
        alpha = [jnp.exp2(m[h] - m_new[h]) for h in hs]
        p = [jnp.exp2(s[h] - m_new[h]).astype(_bf16) for h in hs]
        acc = [alpha[h] * acc[h] + _dot(vt_ref[h, :, k0:k0 + nk], p[h]) for h in hs]
        m = m_new
        if k0 == i * tq:
            for h in hs:
                inv_l = 1.0 / acc[h][HEAD_DIM:HEAD_DIM + 1, :]
                o_ref[rows(i), cols(h)] = (acc[h][0:HEAD_DIM, :] * inv_l).T.astype(o_ref.dtype)


def _fox(proj, negc, B, S, *, tq):
    T = proj.shape[0]
    hp = FOX_HEADS_PER_STEP
    width = hp * HEAD_DIM
    return pl.pallas_call(
        functools.partial(_fox_kernel, tq=tq),
        grid=(B, HEADS // hp),
        in_specs=[
            pl.BlockSpec((S, width), lambda b, h: (b, COL_FQ * (HEADS // hp) + h)),
            pl.BlockSpec((S, width), lambda b, h: (b, COL_FK * (HEADS // hp) + h)),
            pl.BlockSpec((S, width), lambda b, h: (b, COL_FV * (HEADS // hp) + h)),
            pl.BlockSpec((None, hp, S, LANES), lambda b, h: (b, h, 0, 0)),
        ],
        out_specs=pl.BlockSpec((S, width), lambda b, h: (b, h)),
        out_shape=jax.ShapeDtypeStruct((T, WIDTH), _bf16),
        scratch_shapes=[pltpu.VMEM((hp, FOX_VT_ROWS, S), _bf16)],
        compiler_params=_params(("arbitrary", "arbitrary")),
        name="fox",
    )(proj, proj, proj, negc)


def _gdn_kernel(q_ref, k_ref, v_ref, z_ref, g_ref, nw_ref, o_ref, cbuf_ref, state_ref, *, lc):
    C = GDN_CHUNK
    P = GDN_GROUP
    npairs = lc // P
    nc = P // C
    step = pl.program_id(1)
    srcs = (q_ref, k_ref, v_ref)

    @pl.when(step == 0)
    def _():
        state_ref[...] = jnp.zeros(state_ref.shape, _f32)

    ri = lax.broadcasted_iota(jnp.int32, (C, C), 0)
    ci = lax.broadcasted_iota(jnp.int32, (C, C), 1)
    incl = ci <= ri
    strict = ci < ri
    diag = ci == ri
    scale = HEAD_DIM ** -0.5
    nw = nw_ref[...]

    def act_unit(cb, p1, slot):
        c0 = (cb % HEADS) * HEAD_DIM
        act = _silu(srcs[cb // HEADS][pl.ds(p1, P), c0:c0 + HEAD_DIM].astype(_f32))
        if cb < 2 * HEADS:
            act = act * lax.rsqrt(jnp.sum(act * act, axis=-1, keepdims=True) + EPS)
        cbuf_ref[slot, :, cb * HEAD_DIM:(cb + 1) * HEAD_DIM] = act

    for cb in range(3 * HEADS):
        act_unit(cb, 0, 0)

    def pair_body(n, carry):
        p0 = pl.multiple_of(n * P, P)
        slot = lax.rem(n, 2)
        p1 = pl.multiple_of(jnp.minimum(n + 1, npairs - 1) * P, P)
        pending = iter(range(3 * HEADS))

        def emit(count):
            for _ in range(count):
                cb = next(pending, None)
                if cb is not None:
                    act_unit(cb, p1, 1 - slot)

        gpair = g_ref[pl.ds(p0, P), :]
        gpair_t = gpair.T
        items = [(c, h) for c in range(nc) for h in range(HEADS)]
        it = range(len(items))
        crow = [slice(c * C, (c + 1) * C) for c, h in items]
        hcol = [slice(h * HEAD_DIM, (h + 1) * HEAD_DIM) for c, h in items]
        q = [cbuf_ref[slot, crow[x], hcol[x]] for x in it]
        k = [cbuf_ref[slot, crow[x], WIDTH + items[x][1] * HEAD_DIM:WIDTH + (items[x][1] + 1) * HEAD_DIM]
             for x in it]
        v = [cbuf_ref[slot, crow[x], 2 * WIDTH + items[x][1] * HEAD_DIM:2 * WIDTH + (items[x][1] + 1) * HEAD_DIM]
             for x in it]
        gcol = [gpair[c * C:(c + 1) * C, LANE_GA + h:LANE_GA + h + 1] for c, h in items]
        bcol = [gpair[c * C:(c + 1) * C, LANE_GB + h:LANE_GB + h + 1] for c, h in items]
        grow = [gpair_t[LANE_GA + h:LANE_GA + h + 1, c * C:(c + 1) * C] for c, h in items]
        glast = [gcol[x][C - 1:C, :] for x in it]
        decay = [jnp.exp(jnp.where(incl, gcol[x] - grow[x], NEG_BIG)) for x in it]
        e_col = [jnp.exp(gcol[x]) for x in it]
        emit(2)
        kb = [k[x] * bcol[x] for x in it]
        k16 = [k[x].astype(_bf16) for x in it]
        a = [jnp.where(strict, _dot_nt(kb[x].astype(_bf16), k16[x]) * decay[x], 0.0) for x in it]
        attn = [(_dot_nt((q[x] * scale).astype(_bf16), k16[x]) * decay[x]).astype(_bf16) for x in it]
        emit(2)
        rhs = [jnp.concatenate([v[x] * bcol[x], kb[x] * e_col[x]], axis=-1).astype(_bf16) for x in it]
        q_dec = [(q[x] * (scale * e_col[x])).astype(_bf16) for x in it]
        k_dec = [(k[x] * jnp.exp(glast[x] - gcol[x])).astype(_bf16) for x in it]
        emit(2)
        xp = [-a[x] for x in it]
        tinv = [jnp.where(diag, 1.0, xp[x]) for x in it]
        for level in range(5):
            x16 = [xp[x].astype(_bf16) for x in it]
            xp = [_dot(x16[x], x16[x]) for x in it]
            if level > 0:
                tinv = [tinv[x] + _dot(tinv[x].astype(_bf16), x16[x]) for x in it]
            emit(2)
        tinv = [tinv[x] + _dot(tinv[x].astype(_bf16), xp[x].astype(_bf16)) for x in it]
        sol = [_dot(tinv[x].astype(_bf16), rhs[x]) for x in it]
        u_hat = [sol[x][:, :HEAD_DIM] for x in it]
        w16 = [sol[x][:, HEAD_DIM:].astype(_bf16) for x in it]
        emit(2)
        hs = range(HEADS)
        st = [state_ref[h] for h in hs]
        for c in range(nc):
            xs = [c * HEADS + h for h in hs]
            rows = pl.ds(p0 + c * C, C)
            st16 = [st[h].astype(_bf16) for h in hs]
            u = [u_hat[xs[h]] - _dot(w16[xs[h]], st16[h]) for h in hs]
            u16 = [u[h].astype(_bf16) for h in hs]
            emit(1)
            o = [_dot(q_dec[xs[h]], st16[h]) + _dot(attn[xs[h]], u16[h]) for h in hs]
            emit(1)
            st = [st[h] * jnp.exp(glast[xs[h]]) + _dot_tn(k_dec[xs[h]], u16[h]) for h in hs]
            emit(1)
            for h in hs:
                on = o[h] * lax.rsqrt(jnp.mean(o[h] * o[h], axis=-1, keepdims=True) + EPS) * nw
                zg = z_ref[rows, hcol[h]].astype(_f32)
                o_ref[rows, hcol[h]] = (on * _silu(zg)).astype(o_ref.dtype)
        emit(3 * HEADS)
        for h in hs:
            state_ref[h] = st[h]
        return carry

    lax.fori_loop(0, npairs, pair_body, 0)


def _gdn(proj, g, norm_w, B, S, *, lc):
    T = proj.shape[0]
    nsteps = S // lc
    row = lambda b, s: b * nsteps + s
    return pl.pallas_call(
        functools.partial(_gdn_kernel, lc=lc),
        grid=(B, nsteps),
        in_specs=[
            pl.BlockSpec((lc, WIDTH), lambda b, s: (row(b, s), COL_GQ)),
            pl.BlockSpec((lc, WIDTH), lambda b, s: (row(b, s), COL_GK)),
            pl.BlockSpec((lc, WIDTH), lambda b, s: (row(b, s), COL_GV)),
            pl.BlockSpec((lc, WIDTH), lambda b, s: (row(b, s), COL_GZ)),
            pl.BlockSpec((lc, LANES), lambda b, s: (row(b, s), 0)),
            pl.BlockSpec((1, HEAD_DIM), lambda b, s: (0, 0)),
        ],
        out_specs=pl.BlockSpec((lc, WIDTH), lambda b, s: (row(b, s), 0)),
        out_shape=jax.ShapeDtypeStruct((T, WIDTH), _bf16),
        scratch_shapes=[
            pltpu.VMEM((2, GDN_GROUP, 3 * WIDTH), _f32),
            pltpu.VMEM((HEADS, HEAD_DIM, HEAD_DIM), _f32),
        ],
        compiler_params=_params(("arbitrary", "arbitrary")),
        name="gdn",
    )(proj, proj, proj, proj, g, norm_w)


def _merge_kernel(x_ref, yf_ref, yg_ref, gf_ref, gg_ref, wf_ref, wg_ref, wo_ref, o_ref):
    a = _sigmoid(gf_ref[...].astype(_f32)) * _dot(yf_ref[...], wf_ref[...])
    b = _sigmoid(gg_ref[...].astype(_f32)) * _dot(yg_ref[...], wg_ref[...])
    y = (a + b).astype(_bf16)
    o_ref[...] = x_ref[...] + _dot(y, wo_ref[...])


def _merge(x2, y_fox, y_gdn, proj, wf, wg, wo, *, tm):
    T, D = x2.shape
    rows = lambda i: (i, 0)
    const = lambda i: (0, 0)
    return pl.pallas_call(
        _merge_kernel,
        grid=(T // tm,),
        in_specs=[
            pl.BlockSpec((tm, D), rows),
            pl.BlockSpec((tm, WIDTH), rows),
            pl.BlockSpec((tm, WIDTH), rows),
            pl.BlockSpec((tm, D), lambda i: (i, COL_GATE_FOX)),
            pl.BlockSpec((tm, D), lambda i: (i, COL_GATE_GDN)),
            pl.BlockSpec((WIDTH, D), const),
            pl.BlockSpec((WIDTH, D), const),
            pl.BlockSpec((D, D), const),
        ],
        out_specs=pl.BlockSpec((tm, D), rows),
        out_shape=jax.ShapeDtypeStruct((T, D), _f32),
        compiler_params=_params(("arbitrary",)),
        name="merge",
    )(x2, y_fox, y_gdn, proj, proj, wf, wg, wo)


def _ffn_kernel(h_ref, g_ref, wu_ref, cw_ref, wd_ref, gfin_ref, o_ref,
                tail_ref, act_ref, *, tf, final_norm):
    tm = h_ref.shape[0]
    d_ff = wd_ref.shape[0]
    step = pl.program_id(1)

    @pl.when(step == 0)
    def _():
        tail_ref[...] = jnp.zeros(tail_ref.shape, _f32)

    x = h_ref[...]
    hn = (x * lax.rsqrt(jnp.mean(x * x, axis=-1, keepdims=True) + EPS) * g_ref[...]).astype(_bf16)

    def conv(c0):
        up = _dot(hn, wu_ref[:, c0:c0 + tf])
        prev = tail_ref[:, c0:c0 + tf]
        tail_ref[:, c0:c0 + tf] = up[tm - SUBLANES:, :]
        return _causal_conv_rows(up, prev, cw_ref, c0)

    for c in range(d_ff // tf):
        gate = conv(c * tf)
        val = conv(d_ff + c * tf)
        act_ref[:, c * tf:(c + 1) * tf] = (_silu(gate) * val).astype(_bf16)

    h2 = x + _dot(act_ref[...], wd_ref[...])
    if final_norm:
        h2 = h2 * lax.rsqrt(jnp.mean(h2 * h2, axis=-1, keepdims=True) + EPS) * gfin_ref[...]
    o_ref[...] = h2


def _ffn(h1, g, wu, conv_w, wd, gfin, B, S, *, tm, tf, final_norm):
    T, D = h1.shape
    d_ff = wd.shape[0]
    nsteps = S // tm
    rows = lambda b, s: (b * nsteps + s, 0)
    const = lambda b, s: (0, 0)
    single = pl.Buffered(1)
    return pl.pallas_call(
        functools.partial(_ffn_kernel, tf=tf, final_norm=final_norm),
        grid=(B, nsteps),
        in_specs=[
            pl.BlockSpec((tm, D), rows),
            pl.BlockSpec((1, D), const),
            pl.BlockSpec((D, 2 * d_ff), const, pipeline_mode=single),
            pl.BlockSpec((FFN_CONV, 2 * d_ff), const),
            pl.BlockSpec((d_ff, D), const, pipeline_mode=single),
            pl.BlockSpec((1, D), const),
        ],
        out_specs=pl.BlockSpec((tm, D), rows),
        out_shape=jax.ShapeDtypeStruct((T, D), _f32),
        scratch_shapes=[
            pltpu.VMEM((SUBLANES, 2 * d_ff), _f32),
            pltpu.VMEM((tm, d_ff), _bf16),
        ],
        compiler_params=_params(("arbitrary", "arbitrary")),
        name="ffn",
    )(h1, g, wu, conv_w, wd, gfin)


def _tiles(S):
    pick = lambda pref: next(t for t in pref if S % t == 0)
    return dict(
        tm_in=pick((1024, 512, 256, 128)),
        tq=pick((256, 128)),
        lc=pick((1024, 512, 256, 128)),
        tm_merge=pick((1024, 512, 256, 128)),
        tm_ffn=pick((1024, 512, 256, 128)),
    )


def kernel(x, norm_mix, w_in, fox_f_bias, gdn_conv_w, gdn_a_log, gdn_dt_bias, gdn_norm,
           w_branch_fox, w_branch_gdn, w_out, norm_ffn, w_up, ffn_conv_w, w_down, norm_final):
    B, S, D = x.shape
    L = norm_mix.shape[0]
    T = B * S
    assert D == WIDTH and S % GDN_GROUP == 0
    t = _tiles(S)
    d_ff = w_down.shape[1]
    tf = 256 if d_ff % 256 == 0 else LANES
    h = x.reshape(T, D)
    o = 0
    seg = {}
    for name, width in (("fq", WIDTH), ("fk", WIDTH), ("fv", WIDTH), ("ff", HEADS),
                        ("gq", WIDTH), ("gk", WIDTH), ("gv", WIDTH), ("ga", HEADS), ("gb", HEADS),
                        ("gz", WIDTH), ("gate_fox", D), ("gate_gdn", D)):
        seg[name] = (o, o + width)
        o += width
    for l in range(L):
        w = w_in[l]
        cols = lambda n: w[:, seg[n][0]:seg[n][1]]
        w_big = jnp.concatenate(
            [cols("fq") * (HEAD_DIM ** -0.5 * LOG2E), cols("fk"), cols("fv"), cols("gq"), cols("gk"), cols("gv"),
             cols("gz"), cols("gate_fox"), cols("gate_gdn")], axis=1).astype(_bf16)
        w_small = jnp.concatenate(
            [cols("ff"), cols("ga"), cols("gb"), jnp.zeros((D, LANES - 3 * HEADS), w.dtype)], axis=1).astype(_bf16)
        zpad = jnp.zeros((LANES - 2 * HEADS,), _f32)
        gate_params = jnp.zeros((SUBLANES, LANES), _f32)
        gate_params = gate_params.at[0].set(jnp.concatenate([fox_f_bias[l], gdn_dt_bias[l], zpad]))
        gate_params = gate_params.at[1].set(jnp.concatenate([jnp.zeros((HEADS,), _f32), gdn_a_log[l], zpad]))

        proj, g, negc = _in_proj(h, norm_mix[l][None, :], w_big, w_small, gdn_conv_w[l], gate_params, B, S,
                                 tm=t["tm_in"])
        y_fox = _fox(proj, negc, B, S, tq=t["tq"])
        y_gdn = _gdn(proj, g, gdn_norm[l][None, :], B, S, lc=t["lc"])
        h1 = _merge(h, y_fox, y_gdn, proj, w_branch_fox[l].astype(_bf16), w_branch_gdn[l].astype(_bf16),
                    w_out[l].astype(_bf16), tm=t["tm_merge"])
        h = _ffn(h1, norm_ffn[l][None, :], w_up[l].astype(_bf16), ffn_conv_w[l], w_down[l].astype(_bf16),
                 norm_final[None, :], B, S, tm=t["tm_ffn"], tf=tf, final_norm=(l == L - 1))
    if L == 0:
        raise ValueError("at least one layer expected")
    return h.reshape(B, S, D)
```

```python
import functools

import jax
import jax.numpy as jnp
from jax import lax
from jax.experimental import pallas as pl
from jax.experimental.pallas import tpu as pltpu

EPS = 1e-6
HEADS = 8
HEAD_DIM = 128
WIDTH = HEADS * HEAD_DIM
GDN_CONV = 4
GDN_CHUNK = 64
GDN_GROUP = 128
FFN_CONV = 3
LANES = 128
SUBLANES = 8
NEG_BIG = -1e30
LOG2E = 1.4426950408889634
FOX_HEADS_PER_STEP = 2
FOX_WIDE_KEYS = 256
FOX_VT_ROWS = HEAD_DIM + 16

COL_FQ, COL_FK, COL_FV, COL_GQ, COL_GK, COL_GV, COL_GZ, COL_GATE_FOX, COL_GATE_GDN = range(9)
N_COLBLOCKS = 9
N_COLSTEPS = 3
COLSTEP_GDN = 1
COLSTEP_GATES = 2
IN_PROJ_SUB = 768
CONV_SUB = 256
LANE_FF, LANE_GA, LANE_GB = 0, HEADS, 2 * HEADS

VMEM_LIMIT = 56 * 1024 * 1024

_f32 = jnp.float32
_bf16 = jnp.bfloat16


def _dot(a, b):
    return jnp.dot(a, b, preferred_element_type=_f32)


def _dot_nt(a, b):
    return lax.dot_general(a, b, (((1,), (1,)), ((), ())), preferred_element_type=_f32)


def _dot_tn(a, b):
    return lax.dot_general(a, b, (((0,), (0,)), ((), ())), preferred_element_type=_f32)


def _sigmoid(z):
    return 1.0 / (1.0 + jnp.exp(-z))


def _silu(z):
    return z * _sigmoid(z)


def _causal_conv_rows(u, prev, w_ref, c0):
    taps = w_ref.shape[0]
    width = u.shape[1]
    first = lax.broadcasted_iota(jnp.int32, (SUBLANES, width), 0) == 0
    acc = acc_prev = None
    for i in range(taps):
        wi = w_ref[i:i + 1, c0:c0 + width]
        if i == 0:
            acc, acc_prev = u * wi, prev * wi
            continue
        rolled = pltpu.roll(acc, 1, axis=0)
        head = jnp.where(first, acc_prev[SUBLANES - 1:SUBLANES, :], rolled[0:SUBLANES, :])
        acc = u * wi + jnp.concatenate([head, rolled[SUBLANES:, :]], axis=0)
        acc_prev = prev * wi + pltpu.roll(acc_prev, 1, axis=0)
    return acc


def _params(sem, vmem=VMEM_LIMIT):
    return pltpu.CompilerParams(dimension_semantics=sem, vmem_limit_bytes=vmem)


def _split3(v):
    hi = v.astype(_bf16)
    r1 = v - hi.astype(_f32)
    mid = r1.astype(_bf16)
    lo = (r1 - mid.astype(_f32)).astype(_bf16)
    return hi, mid, lo


def _in_proj_kernel(x_ref, g_ref, wb_ref, ws_ref, cw_ref, p_ref, proj_ref, gate_ref, nb_ref,
                    hn_ref, small_ref, ubuf_ref, tail_ref, carry_ref, *, tiles_per_seq):
    i = pl.program_id(0)
    j = pl.program_id(1)
    tm = x_ref.shape[0]
    tn = wb_ref.shape[1]
    first_tile = lax.rem(i, tiles_per_seq) == 0

    @pl.when(j == 0)
    def _():
        x = x_ref[...]
        ms = jnp.mean(x * x, axis=-1, keepdims=True)
        hn = (x * lax.rsqrt(ms + EPS) * g_ref[...]).astype(_bf16)
        hn_ref[...] = hn
        small_ref[...] = _dot(hn, ws_ref[...])

    def plain_step(between=None):
        hn = hn_ref[...]
        for n, c0 in enumerate(range(0, tn, IN_PROJ_SUB)):
            proj_ref[:, c0:c0 + IN_PROJ_SUB] = _dot(hn, wb_ref[:, c0:c0 + IN_PROJ_SUB]).astype(_bf16)
            if between is not None:
                between(n)

    @pl.when(j == 0)
    def _():
        plain_step()

    @pl.when((j == COLSTEP_GDN) & first_tile)
    def _():
        tail_ref[...] = jnp.zeros(tail_ref.shape, _f32)

    @pl.when(j == COLSTEP_GDN)
    def _():
        hn = hn_ref[...]
        w = CONV_SUB
        for c in range(tn // w):
            c0 = c * w
            lo = (c % 2) * w
            up = _dot(hn, wb_ref[:, c0:c0 + w])
            ubuf_ref[0:SUBLANES, lo:lo + w] = tail_ref[:, c0:c0 + w]
            ubuf_ref[SUBLANES:, lo:lo + w] = up
            tail_ref[:, c0:c0 + w] = up[tm - SUBLANES:, :]
            acc = None
            for t in range(GDN_CONV):
                off = SUBLANES - (GDN_CONV - 1) + t
                term = ubuf_ref[off:off + tm, lo:lo + w] * cw_ref[t:t + 1, c0:c0 + w]
                acc = term if acc is None else acc + term
            proj_ref[:, c0:c0 + w] = acc.astype(_bf16)

    @pl.when((j == COLSTEP_GATES) & first_tile)
    def _():
        carry_ref[...] = jnp.zeros(carry_ref.shape, _f32)

    @pl.when(j == COLSTEP_GATES)
    def _():
        lane = lax.broadcasted_iota(jnp.int32, (LANES, LANES), 1)
        row = lax.broadcasted_iota(jnp.int32, (LANES, LANES), 0)
        tri = (row >= lane)
        tri_full = jnp.where(tri, 1.0, 0.0).astype(_bf16)
        same_chunk = (row >= GDN_CHUNK) == (lane >= GDN_CHUNK)
        tri_chunk = jnp.where(tri & same_chunk, 1.0, 0.0).astype(_bf16)
        bias = p_ref[0:1, :]
        neg_a = -jnp.exp(p_ref[1:2, :])
        is_f = lane < LANE_GA
        is_g = lane < LANE_GB

        def gate_block(r):
            rows = slice(r * LANES, (r + 1) * LANES)
            z = small_ref[rows, :] + bias
            t = jnp.log(1.0 + jnp.exp(-jnp.abs(z)))
            logsig = jnp.minimum(z, 0.0) - t
            softplus = jnp.maximum(z, 0.0) + t
            vals = jnp.where(is_f, logsig, jnp.where(is_g, neg_a * softplus, _sigmoid(z)))
            hi, mid, lo = _split3(vals)
            cum_full = (_dot(tri_full, hi) + _dot(tri_full, mid)) + _dot(tri_full, lo) + carry_ref[0:1, :]
            cum_chunk = (_dot(tri_chunk, hi) + _dot(tri_chunk, mid)) + _dot(tri_chunk, lo)
            carry_ref[...] = jnp.broadcast_to(cum_full[LANES - 1:LANES, :], carry_ref.shape)
            out = jnp.where(is_f, cum_full, jnp.where(is_g, cum_chunk, vals))
            gate_ref[rows, :] = out
            for h in range(HEADS):
                nb_ref[h, rows, :] = jnp.broadcast_to(out[:, LANE_FF + h:LANE_FF + h + 1] * (-LOG2E),
                                                      (LANES, LANES))

        nblk = tm // LANES
        nsub = tn // IN_PROJ_SUB
        per = -(-nblk // nsub)

        def between(n):
            for r in range(n * per, min((n + 1) * per, nblk)):
                gate_block(r)

        plain_step(between)


def _in_proj(x2, g, w_big, w_small, conv_w, gate_params, B, S, *, tm):
    T, D = x2.shape
    N = w_big.shape[1]
    tn = N // N_COLSTEPS
    tps = S // tm
    return pl.pallas_call(
        functools.partial(_in_proj_kernel, tiles_per_seq=tps),
        grid=(T // tm, N_COLSTEPS),
        in_specs=[
            pl.BlockSpec((tm, D), lambda i, j: (i, 0)),
            pl.BlockSpec((1, D), lambda i, j: (0, 0)),
            pl.BlockSpec((D, tn), lambda i, j: (0, j)),
            pl.BlockSpec((D, LANES), lambda i, j: (0, 0)),
            pl.BlockSpec((GDN_CONV, tn), lambda i, j: (0, 0)),
            pl.BlockSpec((SUBLANES, LANES), lambda i, j: (0, 0)),
        ],
        out_specs=[
            pl.BlockSpec((tm, tn), lambda i, j: (i, j)),
            pl.BlockSpec((tm, LANES), lambda i, j: (i, 0)),
            pl.BlockSpec((None, HEADS, tm, LANES), lambda i, j: (i // tps, 0, i % tps, 0)),
        ],
        out_shape=[
            jax.ShapeDtypeStruct((T, N), _bf16),
            jax.ShapeDtypeStruct((T, LANES), _f32),
            jax.ShapeDtypeStruct((B, HEADS, S, LANES), _f32),
        ],
        scratch_shapes=[
            pltpu.VMEM((tm, D), _bf16),
            pltpu.VMEM((tm, LANES), _f32),
            pltpu.VMEM((tm + SUBLANES, 2 * CONV_SUB), _f32),
            pltpu.VMEM((SUBLANES, tn), _f32),
            pltpu.VMEM((SUBLANES, LANES), _f32),
        ],
        compiler_params=_params(("arbitrary", "arbitrary")),
        name="in_proj",
    )(x2, g, w_big, w_small, conv_w, gate_params)


def _fox_kernel(q_ref, k_ref, v_ref, nb_ref, o_ref, vt_ref, *, tq):
    S = q_ref.shape[0]
    nq = S // tq
    hs = range(FOX_HEADS_PER_STEP)
    ri = lax.broadcasted_iota(jnp.int32, (tq, tq), 0)
    ci = lax.broadcasted_iota(jnp.int32, (tq, tq), 1)
    visible = ri <= ci
    eye = jnp.where(lax.broadcasted_iota(jnp.int32, (HEAD_DIM, HEAD_DIM), 0)
                    == lax.broadcasted_iota(jnp.int32, (HEAD_DIM, HEAD_DIM), 1), 1.0, 0.0).astype(_bf16)
    cols = lambda h: slice(h * HEAD_DIM, (h + 1) * HEAD_DIM)
    rows = lambda j: slice(j * tq, (j + 1) * tq)
    ones_row = jnp.where(lax.broadcasted_iota(jnp.int32, (FOX_VT_ROWS - HEAD_DIM, S), 0) == 0, 1.0, 0.0)
    for h in hs:
        vt_ref[h, HEAD_DIM:, :] = ones_row.astype(_bf16)
        for j in range(nq):
            vt_ref[h, 0:HEAD_DIM, rows(j)] = _dot_nt(eye, v_ref[rows(j), cols(h)]).astype(_bf16)

    def score(i, k0, nk, h):
        keys = slice(k0, k0 + nk)
        s = _dot_nt(k_ref[keys, cols(h)], q_ref[rows(i), cols(h)])
        nb = nb_ref[h, keys, :]
        s = s + jnp.concatenate([nb] * (tq // LANES), axis=-1)
        if k0 == i * tq:
            s = jnp.where(visible, s, NEG_BIG)
        return s

    tiles = []
    for i in range(nq):
        k0 = 0
        while k0 < i * tq:
            nk = min(FOX_WIDE_KEYS, i * tq - k0)
            tiles.append((i, k0, nk))
            k0 += nk
        tiles.append((i, i * tq, tq))

    s_next = [score(*tiles[0], h) for h in hs]
    for t, (i, k0, nk) in enumerate(tiles):
        s = s_next
        if t + 1 < len(tiles):
            s_next = [score(*tiles[t + 1], h) for h in hs]
        if k0 == 0:
            m = [jnp.full((1, tq), NEG_BIG, _f32) for h in hs]
            acc = [jnp.zeros((FOX_VT_ROWS, tq), _f32) for h in hs]
        m_new = [jnp.maximum(m[h], jnp.max(s[h], axis=0, keepdims=True)) for h in hs]
        alpha = [jnp.exp2(m[h] - m_new[h]) for h in hs]
        p = [jnp.exp2(s[h] - m_new[h]).astype(_bf16) for h in hs]
        acc = [alpha[h] * acc[h] + _dot(vt_ref[h, :, k0:k0 + nk], p[h]) for h in hs]
        m = m_new
        if k0 == i * tq:
            for h in hs:
                inv_l = 1.0 / acc[h][HEAD_DIM:HEAD_DIM + 1, :]
                o_ref[rows(i), cols(h)] = (acc[h][0:HEAD_DIM, :] * inv_l).T.astype(o_ref.dtype)


def _fox(proj, negc, B, S, *, tq):
    T = proj.shape[0]
    hp = FOX_HEADS_PER_STEP
    width = hp * HEAD_DIM
    return pl.pallas_call(
        functools.partial(_fox_kernel, tq=tq),
        grid=(B, HEADS // hp),
        in_specs=[
            pl.BlockSpec((S, width), lambda b, h: (b, COL_FQ * (HEADS // hp) + h)),
            pl.BlockSpec((S, width), lambda b, h: (b, COL_FK * (HEADS // hp) + h)),
            pl.BlockSpec((S, width), lambda b, h: (b, COL_FV * (HEADS // hp) + h)),
            pl.BlockSpec((None, hp, S, LANES), lambda b, h: (b, h, 0, 0)),
        ],
        out_specs=pl.BlockSpec((S, width), lambda b, h: (b, h)),
        out_shape=jax.ShapeDtypeStruct((T, WIDTH), _bf16),
        scratch_shapes=[pltpu.VMEM((hp, FOX_VT_ROWS, S), _bf16)],
        compiler_params=_params(("arbitrary", "arbitrary")),
        name="fox",
    )(proj, proj, proj, negc)


def _gdn_kernel(q_ref, k_ref, v_ref, z_ref, g_ref, nw_ref, o_ref, cbuf_ref, state_ref, *, lc):
    C = GDN_CHUNK
    P = GDN_GROUP
    npairs = lc // P
    nc = P // C
    step = pl.program_id(1)
    srcs = (q_ref, k_ref, v_ref)

    @pl.when(step == 0)
    def _():
        state_ref[...] = jnp.zeros(state_ref.shape, _f32)

    ri = lax.broadcasted_iota(jnp.int32, (C, C), 0)
    ci = lax.broadcasted_iota(jnp.int32, (C, C), 1)
    incl = ci <= ri
    strict = ci < ri
    diag = ci == ri
    scale = HEAD_DIM ** -0.5
    nw = nw_ref[...]

    def act_unit(cb, p1, slot):
        c0 = (cb % HEADS) * HEAD_DIM
        act = _silu(srcs[cb // HEADS][pl.ds(p1, P), c0:c0 + HEAD_DIM].astype(_f32))
        if cb < 2 * HEADS:
            act = act * lax.rsqrt(jnp.sum(act * act, axis=-1, keepdims=True) + EPS)
        cbuf_ref[slot, :, cb * HEAD_DIM:(cb + 1) * HEAD_DIM] = act

    for cb in range(3 * HEADS):
        act_unit(cb, 0, 0)

    def pair_body(n, carry):
        p0 = pl.multiple_of(n * P, P)
        slot = lax.rem(n, 2)
        p1 = pl.multiple_of(jnp.minimum(n + 1, npairs - 1) * P, P)
        pending = iter(range(3 * HEADS))

        def emit(count):
            for _ in range(count):
                cb = next(pending, None)
                if cb is not None:
                    act_unit(cb, p1, 1 - slot)

        gpair = g_ref[pl.ds(p0, P), :]
        gpair_t = gpair.T
        items = [(c, h) for c in range(nc) for h in range(HEADS)]
        it = range(len(items))
        crow = [slice(c * C, (c + 1) * C) for c, h in items]
        hcol = [slice(h * HEAD_DIM, (h + 1) * HEAD_DIM) for c, h in items]
        q = [cbuf_ref[slot, crow[x], hcol[x]] for x in it]
        k = [cbuf_ref[slot, crow[x], WIDTH + items[x][1] * HEAD_DIM:WIDTH + (items[x][1] + 1) * HEAD_DIM]
             for x in it]
        v = [cbuf_ref[slot, crow[x], 2 * WIDTH + items[x][1] * HEAD_DIM:2 * WIDTH + (items[x][1] + 1) * HEAD_DIM]
             for x in it]
        gcol = [gpair[c * C:(c + 1) * C, LANE_GA + h:LANE_GA + h + 1] for c, h in items]
        bcol = [gpair[c * C:(c + 1) * C, LANE_GB + h:LANE_GB + h + 1] for c, h in items]
        grow = [gpair_t[LANE_GA + h:LANE_GA + h + 1, c * C:(c + 1) * C] for c, h in items]
        glast = [gcol[x][C - 1:C, :] for x in it]
        decay = [jnp.exp(jnp.where(incl, gcol[x] - grow[x], NEG_BIG)) for x in it]
        e_col = [jnp.exp(gcol[x]) for x in it]
        emit(2)
        kb = [k[x] * bcol[x] for x in it]
        k16 = [k[x].astype(_bf16) for x in it]
        a = [jnp.where(strict, _dot_nt(kb[x].astype(_bf16), k16[x]) * decay[x], 0.0) for x in it]
        attn = [(_dot_nt((q[x] * scale).astype(_bf16), k16[x]) * decay[x]).astype(_bf16) for x in it]
        emit(2)
        rhs = [jnp.concatenate([v[x] * bcol[x], kb[x] * e_col[x]], axis=-1).astype(_bf16) for x in it]
        q_dec = [(q[x] * (scale * e_col[x])).astype(_bf16) for x in it]
        k_dec = [(k[x] * jnp.exp(glast[x] - gcol[x])).astype(_bf16) for x in it]
        emit(2)
        xp = [-a[x] for x in it]
        tinv = [jnp.where(diag, 1.0, xp[x]) for x in it]
        for level in range(5):
            x16 = [xp[x].astype(_bf16) for x in it]
            xp = [_dot(x16[x], x16[x]) for x in it]
            if level > 0:
                tinv = [tinv[x] + _dot(tinv[x].astype(_bf16), x16[x]) for x in it]
            emit(2)
        tinv = [tinv[x] + _dot(tinv[x].astype(_bf16), xp[x].astype(_bf16)) for x in it]
        sol = [_dot(tinv[x].astype(_bf16), rhs[x]) for x in it]
        u_hat = [sol[x][:, :HEAD_DIM] for x in it]
        w16 = [sol[x][:, HEAD_DIM:].astype(_bf16) for x in it]
        emit(2)
        hs = range(HEADS)
        st = [state_ref[h] for h in hs]
        for c in range(nc):
            xs = [c * HEADS + h for h in hs]
            rows = pl.ds(p0 + c * C, C)
            st16 = [st[h].astype(_bf16) for h in hs]
            u = [u_hat[xs[h]] - _dot(w16[xs[h]], st16[h]) for h in hs]
            u16 = [u[h].astype(_bf16) for h in hs]
            emit(1)
            o = [_dot(q_dec[xs[h]], st16[h]) + _dot(attn[xs[h]], u16[h]) for h in hs]
            emit(1)
            st = [st[h] * jnp.exp(glast[xs[h]]) + _dot_tn(k_dec[xs[h]], u16[h]) for h in hs]
            emit(1)
            for h in hs:
                on = o[h] * lax.rsqrt(jnp.mean(o[h] * o[h], axis=-1, keepdims=True) + EPS) * nw
                zg = z_ref[rows, hcol[h]].astype(_f32)
                o_ref[rows, hcol[h]] = (on * _silu(zg)).astype(o_ref.dtype)
        emit(3 * HEADS)
        for h in hs:
            state_ref[h] = st[h]
        return carry

    lax.fori_loop(0, npairs, pair_body, 0)


def _gdn(proj, g, norm_w, B, S, *, lc):
    T = proj.shape[0]
    nsteps = S // lc
    row = lambda b, s: b * nsteps + s
    return pl.pallas_call(
        functools.partial(_gdn_kernel, lc=lc),
        grid=(B, nsteps),
        in_specs=[
            pl.BlockSpec((lc, WIDTH), lambda b, s: (row(b, s), COL_GQ)),
            pl.BlockSpec((lc, WIDTH), lambda b, s: (row(b, s), COL_GK)),
            pl.BlockSpec((lc, WIDTH), lambda b, s: (row(b, s), COL_GV)),
            pl.BlockSpec((lc, WIDTH), lambda b, s: (row(b, s), COL_GZ)),
            pl.BlockSpec((lc, LANES), lambda b, s: (row(b, s), 0)),
            pl.BlockSpec((1, HEAD_DIM), lambda b, s: (0, 0)),
        ],
        out_specs=pl.BlockSpec((lc, WIDTH), lambda b, s: (row(b, s), 0)),
        out_shape=jax.ShapeDtypeStruct((T, WIDTH), _bf16),
        scratch_shapes=[
            pltpu.VMEM((2, GDN_GROUP, 3 * WIDTH), _f32),
            pltpu.VMEM((HEADS, HEAD_DIM, HEAD_DIM), _f32),
        ],
        compiler_params=_params(("arbitrary", "arbitrary")),
        name="gdn",
    )(proj, proj, proj, proj, g, norm_w)


def _merge_kernel(x_ref, yf_ref, yg_ref, gf_ref, gg_ref, wf_ref, wg_ref, wo_ref, o_ref):
    a = _sigmoid(gf_ref[...].astype(_f32)) * _dot(yf_ref[...], wf_ref[...])
    b = _sigmoid(gg_ref[...].astype(_f32)) * _dot(yg_ref[...], wg_ref[...])
    y = (a + b).astype(_bf16)
    o_ref[...] = x_ref[...] + _dot(y, wo_ref[...])


def _merge(x2, y_fox, y_gdn, proj, wf, wg, wo, *, tm):
    T, D = x2.shape
    rows = lambda i: (i, 0)
    const = lambda i: (0, 0)
    return pl.pallas_call(
        _merge_kernel,
        grid=(T // tm,),
        in_specs=[
            pl.BlockSpec((tm, D), rows),
            pl.BlockSpec((tm, WIDTH), rows),
            pl.BlockSpec((tm, WIDTH), rows),
            pl.BlockSpec((tm, D), lambda i: (i, COL_GATE_FOX)),
            pl.BlockSpec((tm, D), lambda i: (i, COL_GATE_GDN)),
            pl.BlockSpec((WIDTH, D), const),
            pl.BlockSpec((WIDTH, D), const),
            pl.BlockSpec((D, D), const),
        ],
        out_specs=pl.BlockSpec((tm, D), rows),
        out_shape=jax.ShapeDtypeStruct((T, D), _f32),
        compiler_params=_params(("arbitrary",)),
        name="merge",
    )(x2, y_fox, y_gdn, proj, proj, wf, wg, wo)


def _ffn_kernel(h_ref, g_ref, wu_ref, cw_ref, wd_ref, gfin_ref, o_ref,
                tail_ref, act_ref, *, tf, final_norm):
    tm = h_ref.shape[0]
    d_ff = wd_ref.shape[0]
    step = pl.program_id(1)

    @pl.when(step == 0)
    def _():
        tail_ref[...] = jnp.zeros(tail_ref.shape, _f32)

    x = h_ref[...]
    hn = (x * lax.rsqrt(jnp.mean(x * x, axis=-1, keepdims=True) + EPS) * g_ref[...]).astype(_bf16)

    def conv(c0):
        up = _dot(hn, wu_ref[:, c0:c0 + tf])
        prev = tail_ref[:, c0:c0 + tf]
        tail_ref[:, c0:c0 + tf] = up[tm - SUBLANES:, :]
        return _causal_conv_rows(up, prev, cw_ref, c0)

    for c in range(d_ff // tf):
        gate = conv(c * tf)
        val = conv(d_ff + c * tf)
        act_ref[:, c * tf:(c + 1) * tf] = (_silu(gate) * val).astype(_bf16)

    h2 = x + _dot(act_ref[...], wd_ref[...])
    if final_norm:
        h2 = h2 * lax.rsqrt(jnp.mean(h2 * h2, axis=-1, keepdims=True) + EPS) * gfin_ref[...]
    o_ref[...] = h2


def _ffn(h1, g, wu, conv_w, wd, gfin, B, S, *, tm, tf, final_norm):
    T, D = h1.shape
    d_ff = wd.shape[0]
    nsteps = S // tm
    rows = lambda b, s: (b * nsteps + s, 0)
    const = lambda b, s: (0, 0)
    single = pl.Buffered(1)
    return pl.pallas_call(
        functools.partial(_ffn_kernel, tf=tf, final_norm=final_norm),
        grid=(B, nsteps),
        in_specs=[
            pl.BlockSpec((tm, D), rows),
            pl.BlockSpec((1, D), const),
            pl.BlockSpec((D, 2 * d_ff), const, pipeline_mode=single),
            pl.BlockSpec((FFN_CONV, 2 * d_ff), const),
            pl.BlockSpec((d_ff, D), const, pipeline_mode=single),
            pl.BlockSpec((1, D), const),
        ],
        out_specs=pl.BlockSpec((tm, D), rows),
        out_shape=jax.ShapeDtypeStruct((T, D), _f32),
        scratch_shapes=[
            pltpu.VMEM((SUBLANES, 2 * d_ff), _f32),
            pltpu.VMEM((tm, d_ff), _bf16),
        ],
        compiler_params=_params(("arbitrary", "arbitrary")),
        name="ffn",
    )(h1, g, wu, conv_w, wd, gfin)


def _tiles(S):
    pick = lambda pref: next(t for t in pref if S % t == 0)
    return dict(
        tm_in=pick((1024, 512, 256, 128)),
        tq=pick((256, 128)),
        lc=pick((2048, 1024, 512, 256, 128)),
        tm_merge=pick((1024, 512, 256, 128)),
        tm_ffn=pick((1024, 512, 256, 128)),
    )


def kernel(x, norm_mix, w_in, fox_f_bias, gdn_conv_w, gdn_a_log, gdn_dt_bias, gdn_norm,
           w_branch_fox, w_branch_gdn, w_out, norm_ffn, w_up, ffn_conv_w, w_down, norm_final):
    B, S, D = x.shape
    L = norm_mix.shape[0]
    T = B * S
    assert D == WIDTH and S % GDN_GROUP == 0
    t = _tiles(S)
    d_ff = w_down.shape[1]
    tf = 256 if d_ff % 256 == 0 else LANES
    h = x.reshape(T, D)
    o = 0
    seg = {}
    for name, width in (("fq", WIDTH), ("fk", WIDTH), ("fv", WIDTH), ("ff", HEADS),
                        ("gq", WIDTH), ("gk", WIDTH), ("gv", WIDTH), ("ga", HEADS), ("gb", HEADS),
                        ("gz", WIDTH), ("gate_fox", D), ("gate_gdn", D)):
        seg[name] = (o, o + width)
        o += width
    for l in range(L):
        w = w_in[l]
        cols = lambda n: w[:, seg[n][0]:seg[n][1]]
        w_big = jnp.concatenate(
            [cols("fq") * (HEAD_DIM ** -0.5 * LOG2E), cols("fk"), cols("fv"), cols("gq"), cols("gk"), cols("gv"),
             cols("gz"), cols("gate_fox"), cols("gate_gdn")], axis=1).astype(_bf16)
        w_small = jnp.concatenate(
            [cols("ff"), cols("ga"), cols("gb"), jnp.zeros((D, LANES - 3 * HEADS), w.dtype)], axis=1).astype(_bf16)
        zpad = jnp.zeros((LANES - 2 * HEADS,), _f32)
        gate_params = jnp.zeros((SUBLANES, LANES), _f32)
        gate_params = gate_params.at[0].set(jnp.concatenate([fox_f_bias[l], gdn_dt_bias[l], zpad]))
        gate_params = gate_params.at[1].set(jnp.concatenate([jnp.zeros((HEADS,), _f32), gdn_a_log[l], zpad]))

        proj, g, negc = _in_proj(h, norm_mix[l][None, :], w_big, w_small, gdn_conv_w[l], gate_params, B, S,
                                 tm=t["tm_in"])
        y_fox = _fox(proj, negc, B, S, tq=t["tq"])
        y_gdn = _gdn(proj, g, gdn_norm[l][None, :], B, S, lc=t["lc"])
        h1 = _merge(h, y_fox, y_gdn, proj, w_branch_fox[l].astype(_bf16), w_branch_gdn[l].astype(_bf16),
                    w_out[l].astype(_bf16), tm=t["tm_merge"])
        h = _ffn(h1, norm_ffn[l][None, :], w_up[l].astype(_bf16), ffn_conv_w[l], w_down[l].astype(_bf16),
                 norm_final[None, :], B, S, tm=t["tm_ffn"], tf=tf, final_norm=(l == L - 1))
    if L == 0:
        raise ValueError("at least one layer expected")
    return h.reshape(B, S, D)
```

```python
import functools

import jax
import jax.numpy as jnp
from jax import lax
from jax.experimental import pallas as pl
from jax.experimental.pallas import tpu as pltpu

EPS = 1e-6
HEADS = 8
HEAD_DIM = 128
WIDTH = HEADS * HEAD_DIM
GDN_CONV = 4
GDN_CHUNK = 64
GDN_GROUP = 128
FFN_CONV = 3
LANES = 128
SUBLANES = 8
NEG_BIG = -1e30
LOG2E = 1.4426950408889634
FOX_HEADS_PER_STEP = 2
FOX_WIDE_KEYS = 256
FOX_VT_ROWS = HEAD_DIM + 16

COL_FQ, COL_FK, COL_FV, COL_GQ, COL_GK, COL_GV, COL_GZ, COL_GATE_FOX, COL_GATE_GDN = range(9)
N_COLBLOCKS = 9
N_COLSTEPS = 3
COLSTEP_GDN = 1
COLSTEP_GATES = 2
IN_PROJ_SUB = 768
CONV_SUB = 256
LANE_FF, LANE_GA, LANE_GB = 0, HEADS, 2 * HEADS

VMEM_LIMIT = 56 * 1024 * 1024

_f32 = jnp.float32
_bf16 = jnp.bfloat16


def _dot(a, b):
    return jnp.dot(a, b, preferred_element_type=_f32)


def _dot_nt(a, b):
    return lax.dot_general(a, b, (((1,), (1,)), ((), ())), preferred_element_type=_f32)


def _dot_tn(a, b):
    return lax.dot_general(a, b, (((0,), (0,)), ((), ())), preferred_element_type=_f32)


def _sigmoid(z):
    return 1.0 / (1.0 + jnp.exp(-z))


def _silu(z):
    return z * _sigmoid(z)


def _causal_conv_rows(u, prev, w_ref, c0):
    taps = w_ref.shape[0]
    width = u.shape[1]
    first = lax.broadcasted_iota(jnp.int32, (SUBLANES, width), 0) == 0
    acc = acc_prev = None
    for i in range(taps):
        wi = w_ref[i:i + 1, c0:c0 + width]
        if i == 0:
            acc, acc_prev = u * wi, prev * wi
            continue
        rolled = pltpu.roll(acc, 1, axis=0)
        head = jnp.where(first, acc_prev[SUBLANES - 1:SUBLANES, :], rolled[0:SUBLANES, :])
        acc = u * wi + jnp.concatenate([head, rolled[SUBLANES:, :]], axis=0)
        acc_prev = prev * wi + pltpu.roll(acc_prev, 1, axis=0)
    return acc


def _params(sem, vmem=VMEM_LIMIT):
    return pltpu.CompilerParams(dimension_semantics=sem, vmem_limit_bytes=vmem)


def _split3(v):
    hi = v.astype(_bf16)
    r1 = v - hi.astype(_f32)
    mid = r1.astype(_bf16)
    lo = (r1 - mid.astype(_f32)).astype(_bf16)
    return hi, mid, lo


def _in_proj_kernel(x_ref, g_ref, wb_ref, ws_ref, cw_ref, p_ref, proj_ref, gate_ref, nb_ref,
                    hn_ref, small_ref, ubuf_ref, tail_ref, carry_ref, *, tiles_per_seq):
    i = pl.program_id(0)
    j = pl.program_id(1)
    tm = x_ref.shape[0]
    tn = wb_ref.shape[1]
    first_tile = lax.rem(i, tiles_per_seq) == 0

    @pl.when(j == 0)
    def _():
        x = x_ref[...]
        ms = jnp.mean(x * x, axis=-1, keepdims=True)
        hn = (x * lax.rsqrt(ms + EPS) * g_ref[...]).astype(_bf16)
        hn_ref[...] = hn
        small_ref[...] = _dot(hn, ws_ref[...])

    def plain_step(between=None):
        hn = hn_ref[...]
        for n, c0 in enumerate(range(0, tn, IN_PROJ_SUB)):
            proj_ref[:, c0:c0 + IN_PROJ_SUB] = _dot(hn, wb_ref[:, c0:c0 + IN_PROJ_SUB]).astype(_bf16)
            if between is not None:
                between(n)

    @pl.when(j == 0)
    def _():
        plain_step()

    @pl.when((j == COLSTEP_GDN) & first_tile)
    def _():
        tail_ref[...] = jnp.zeros(tail_ref.shape, _f32)

    @pl.when(j == COLSTEP_GDN)
    def _():
        hn = hn_ref[...]
        w = CONV_SUB
        for c in range(tn // w):
            c0 = c * w
            lo = (c % 2) * w
            up = _dot(hn, wb_ref[:, c0:c0 + w])
            ubuf_ref[0:SUBLANES, lo:lo + w] = tail_ref[:, c0:c0 + w]
            ubuf_ref[SUBLANES:, lo:lo + w] = up
            tail_ref[:, c0:c0 + w] = up[tm - SUBLANES:, :]
            acc = None
            for t in range(GDN_CONV):
                off = SUBLANES - (GDN_CONV - 1) + t
                term = ubuf_ref[off:off + tm, lo:lo + w] * cw_ref[t:t + 1, c0:c0 + w]
                acc = term if acc is None else acc + term
            proj_ref[:, c0:c0 + w] = acc.astype(_bf16)

    @pl.when((j == COLSTEP_GATES) & first_tile)
    def _():
        carry_ref[...] = jnp.zeros(carry_ref.shape, _f32)

    @pl.when(j == COLSTEP_GATES)
    def _():
        lane = lax.broadcasted_iota(jnp.int32, (LANES, LANES), 1)
        row = lax.broadcasted_iota(jnp.int32, (LANES, LANES), 0)
        tri = (row >= lane)
        tri_full = jnp.where(tri, 1.0, 0.0).astype(_bf16)
        same_chunk = (row >= GDN_CHUNK) == (lane >= GDN_CHUNK)
        tri_chunk = jnp.where(tri & same_chunk, 1.0, 0.0).astype(_bf16)
        bias = p_ref[0:1, :]
        neg_a = -jnp.exp(p_ref[1:2, :])
        is_f = lane < LANE_GA
        is_g = lane < LANE_GB

        def gate_block(r):
            rows = slice(r * LANES, (r + 1) * LANES)
            z = small_ref[rows, :] + bias
            t = jnp.log(1.0 + jnp.exp(-jnp.abs(z)))
            logsig = jnp.minimum(z, 0.0) - t
            softplus = jnp.maximum(z, 0.0) + t
            vals = jnp.where(is_f, logsig, jnp.where(is_g, neg_a * softplus, _sigmoid(z)))
            hi, mid, lo = _split3(vals)
            cum_full = (_dot(tri_full, hi) + _dot(tri_full, mid)) + _dot(tri_full, lo) + carry_ref[0:1, :]
            cum_chunk = (_dot(tri_chunk, hi) + _dot(tri_chunk, mid)) + _dot(tri_chunk, lo)
            carry_ref[...] = jnp.broadcast_to(cum_full[LANES - 1:LANES, :], carry_ref.shape)
            out = jnp.where(is_f, cum_full, jnp.where(is_g, cum_chunk, vals))
            gate_ref[rows, :] = out
            for h in range(HEADS):
                nb_ref[h, rows, :] = jnp.broadcast_to(out[:, LANE_FF + h:LANE_FF + h + 1] * (-LOG2E),
                                                      (LANES, LANES))

        nblk = tm // LANES
        nsub = tn // IN_PROJ_SUB
        per = -(-nblk // nsub)

        def between(n):
            for r in range(n * per, min((n + 1) * per, nblk)):
                gate_block(r)

        plain_step(between)


def _in_proj(x2, g, w_big, w_small, conv_w, gate_params, B, S, *, tm):
    T, D = x2.shape
    N = w_big.shape[1]
    tn = N // N_COLSTEPS
    tps = S // tm
    return pl.pallas_call(
        functools.partial(_in_proj_kernel, tiles_per_seq=tps),
        grid=(T // tm, N_COLSTEPS),
        in_specs=[
            pl.BlockSpec((tm, D), lambda i, j: (i, 0)),
            pl.BlockSpec((1, D), lambda i, j: (0, 0)),
            pl.BlockSpec((D, tn), lambda i, j: (0, j)),
            pl.BlockSpec((D, LANES), lambda i, j: (0, 0)),
            pl.BlockSpec((GDN_CONV, tn), lambda i, j: (0, 0)),
            pl.BlockSpec((SUBLANES, LANES), lambda i, j: (0, 0)),
        ],
        out_specs=[
            pl.BlockSpec((tm, tn), lambda i, j: (i, j)),
            pl.BlockSpec((tm, LANES), lambda i, j: (i, 0)),
            pl.BlockSpec((None, HEADS, tm, LANES), lambda i, j: (i // tps, 0, i % tps, 0)),
        ],
        out_shape=[
            jax.ShapeDtypeStruct((T, N), _bf16),
            jax.ShapeDtypeStruct((T, LANES), _f32),
            jax.ShapeDtypeStruct((B, HEADS, S, LANES), _f32),
        ],
        scratch_shapes=[
            pltpu.VMEM((tm, D), _bf16),
            pltpu.VMEM((tm, LANES), _f32),
            pltpu.VMEM((tm + SUBLANES, 2 * CONV_SUB), _f32),
            pltpu.VMEM((SUBLANES, tn), _f32),
            pltpu.VMEM((SUBLANES, LANES), _f32),
        ],
        compiler_params=_params(("arbitrary", "arbitrary")),
        name="in_proj",
    )(x2, g, w_big, w_small, conv_w, gate_params)


def _fox_kernel(q_ref, k_ref, v_ref, nb_ref, o_ref, vt_ref, *, tq):
    S = q_ref.shape[0]
    nq = S // tq
    hs = range(FOX_HEADS_PER_STEP)
    ri = lax.broadcasted_iota(jnp.int32, (tq, tq), 0)
    ci = lax.broadcasted_iota(jnp.int32, (tq, tq), 1)
    visible = ri <= ci
    eye = jnp.where(lax.broadcasted_iota(jnp.int32, (HEAD_DIM, HEAD_DIM), 0)
                    == lax.broadcasted_iota(jnp.int32, (HEAD_DIM, HEAD_DIM), 1), 1.0, 0.0).astype(_bf16)
    cols = lambda h: slice(h * HEAD_DIM, (h + 1) * HEAD_DIM)
    rows = lambda j: slice(j * tq, (j + 1) * tq)
    ones_row = jnp.where(lax.broadcasted_iota(jnp.int32, (FOX_VT_ROWS - HEAD_DIM, S), 0) == 0, 1.0, 0.0)
    for h in hs:
        vt_ref[h, HEAD_DIM:, :] = ones_row.astype(_bf16)
        for j in range(nq):
            vt_ref[h, 0:HEAD_DIM, rows(j)] = _dot_nt(eye, v_ref[rows(j), cols(h)]).astype(_bf16)

    def score(i, k0, nk, h):
        keys = slice(k0, k0 + nk)
        s = _dot_nt(k_ref[keys, cols(h)], q_ref[rows(i), cols(h)])
        nb = nb_ref[h, keys, :]
        s = s + jnp.concatenate([nb] * (tq // LANES), axis=-1)
        if k0 == i * tq:
            s = jnp.where(visible, s, NEG_BIG)
        return s

    tiles = []
    for i in range(nq):
        k0 = 0
        while k0 < i * tq:
            nk = min(FOX_WIDE_KEYS, i * tq - k0)
            tiles.append((i, k0, nk))
            k0 += nk
        tiles.append((i, i * tq, tq))

    s_next = [score(*tiles[0], h) for h in hs]
    for t, (i, k0, nk) in enumerate(tiles):
        s = s_next
        if t + 1 < len(tiles):
            s_next = [score(*tiles[t + 1], h) for h in hs]
        if k0 == 0:
            m = [jnp.full((1, tq), NEG_BIG, _f32) for h in hs]
            acc = [jnp.zeros((FOX_VT_ROWS, tq), _f32) for h in hs]
        m_new = [jnp.maximum(m[h], jnp.max(s[h], axis=0, keepdims=True)) for h in hs]
        alpha = [jnp.exp2(m[h] - m_new[h]) for h in hs]
        p = [jnp.exp2(s[h] - m_new[h]).astype(_bf16) for h in hs]
        acc = [alpha[h] * acc[h] + _dot(vt_ref[h, :, k0:k0 + nk], p[h]) for h in hs]
        m = m_new
        if k0 == i * tq:
            for h in hs:
                inv_l = 1.0 / acc[h][HEAD_DIM:HEAD_DIM + 1, :]
                o_ref[rows(i), cols(h)] = (acc[h][0:HEAD_DIM, :] * inv_l).T.astype(o_ref.dtype)


def _fox(proj, negc, B, S, *, tq):
    T = proj.shape[0]
    hp = FOX_HEADS_PER_STEP
    width = hp * HEAD_DIM
    return pl.pallas_call(
        functools.partial(_fox_kernel, tq=tq),
        grid=(B, HEADS // hp),
        in_specs=[
            pl.BlockSpec((S, width), lambda b, h: (b, COL_FQ * (HEADS // hp) + h)),
            pl.BlockSpec((S, width), lambda b, h: (b, COL_FK * (HEADS // hp) + h)),
            pl.BlockSpec((S, width), lambda b, h: (b, COL_FV * (HEADS // hp) + h)),
            pl.BlockSpec((None, hp, S, LANES), lambda b, h: (b, h, 0, 0)),
        ],
        out_specs=pl.BlockSpec((S, width), lambda b, h: (b, h)),
        out_shape=jax.ShapeDtypeStruct((T, WIDTH), _bf16),
        scratch_shapes=[pltpu.VMEM((hp, FOX_VT_ROWS, S), _bf16)],
        compiler_params=_params(("arbitrary", "arbitrary")),
        name="fox",
    )(proj, proj, proj, negc)


def _gdn_kernel(q_ref, k_ref, v_ref, z_ref, g_ref, nw_ref, o_ref, cbuf_ref, state_ref, *, lc):
    C = GDN_CHUNK
    P = GDN_GROUP
    npairs = lc // P
    nc = P // C
    step = pl.program_id(1)
    srcs = (q_ref, k_ref, v_ref)

    @pl.when(step == 0)
    def _():
        state_ref[...] = jnp.zeros(state_ref.shape, _f32)

    ri = lax.broadcasted_iota(jnp.int32, (C, C), 0)
    ci = lax.broadcasted_iota(jnp.int32, (C, C), 1)
    incl = ci <= ri
    strict = ci < ri
    diag = ci == ri
    scale = HEAD_DIM ** -0.5
    nw = nw_ref[...]

    def act_unit(cb, p1, slot):
        c0 = (cb % HEADS) * HEAD_DIM
        act = _silu(srcs[cb // HEADS][pl.ds(p1, P), c0:c0 + HEAD_DIM].astype(_f32))
        if cb < 2 * HEADS:
            act = act * lax.rsqrt(jnp.sum(act * act, axis=-1, keepdims=True) + EPS)
        cbuf_ref[slot, :, cb * HEAD_DIM:(cb + 1) * HEAD_DIM] = act

    for cb in range(3 * HEADS):
        act_unit(cb, 0, 0)

    def pair_body(n, carry):
        p0 = pl.multiple_of(n * P, P)
        slot = lax.rem(n, 2)
        p1 = pl.multiple_of(jnp.minimum(n + 1, npairs - 1) * P, P)
        pending = iter(range(3 * HEADS))

        def emit(count):
            for _ in range(count):
                cb = next(pending, None)
                if cb is not None:
                    act_unit(cb, p1, 1 - slot)

        gpair = g_ref[pl.ds(p0, P), :]
        gpair_t = gpair.T
        items = [(c, h) for c in range(nc) for h in range(HEADS)]
        it = range(len(items))
        crow = [slice(c * C, (c + 1) * C) for c, h in items]
        hcol = [slice(h * HEAD_DIM, (h + 1) * HEAD_DIM) for c, h in items]
        q = [cbuf_ref[slot, crow[x], hcol[x]] for x in it]
        k = [cbuf_ref[slot, crow[x], WIDTH + items[x][1] * HEAD_DIM:WIDTH + (items[x][1] + 1) * HEAD_DIM]
             for x in it]
        v = [cbuf_ref[slot, crow[x], 2 * WIDTH + items[x][1] * HEAD_DIM:2 * WIDTH + (items[x][1] + 1) * HEAD_DIM]
             for x in it]
        gcol = [gpair[c * C:(c + 1) * C, LANE_GA + h:LANE_GA + h + 1] for c, h in items]
        bcol = [gpair[c * C:(c + 1) * C, LANE_GB + h:LANE_GB + h + 1] for c, h in items]
        grow = [gpair_t[LANE_GA + h:LANE_GA + h + 1, c * C:(c + 1) * C] for c, h in items]
        glast = [gcol[x][C - 1:C, :] for x in it]
        decay = [jnp.exp(jnp.where(incl, gcol[x] - grow[x], NEG_BIG)) for x in it]
        e_col = [jnp.exp(gcol[x]) for x in it]
        emit(2)
        kb = [k[x] * bcol[x] for x in it]
        k16 = [k[x].astype(_bf16) for x in it]
        a = [jnp.where(strict, _dot_nt(kb[x].astype(_bf16), k16[x]) * decay[x], 0.0) for x in it]
        attn = [(_dot_nt((q[x] * scale).astype(_bf16), k16[x]) * decay[x]).astype(_bf16) for x in it]
        emit(2)
        rhs = [jnp.concatenate([v[x] * bcol[x], kb[x] * e_col[x]], axis=-1).astype(_bf16) for x in it]
        q_dec = [(q[x] * (scale * e_col[x])).astype(_bf16) for x in it]
        k_dec = [(k[x] * jnp.exp(glast[x] - gcol[x])).astype(_bf16) for x in it]
        emit(2)
        xp = [-a[x] for x in it]
        tinv = [jnp.where(diag, 1.0, xp[x]) for x in it]
        for level in range(5):
            x16 = [xp[x].astype(_bf16) for x in it]
            xp = [_dot(x16[x], x16[x]) for x in it]
            if level > 0:
                tinv = [tinv[x] + _dot(tinv[x].astype(_bf16), x16[x]) for x in it]
            emit(2)
        tinv = [tinv[x] + _dot(tinv[x].astype(_bf16), xp[x].astype(_bf16)) for x in it]
        sol = [_dot(tinv[x].astype(_bf16), rhs[x]) for x in it]
        u_hat = [sol[x][:, :HEAD_DIM] for x in it]
        w16 = [sol[x][:, HEAD_DIM:].astype(_bf16) for x in it]
        emit(2)
        hs = range(HEADS)
        st = [state_ref[h] for h in hs]
        for c in range(nc):
            xs = [c * HEADS + h for h in hs]
            rows = pl.ds(p0 + c * C, C)
            st16 = [st[h].astype(_bf16) for h in hs]
            u = [u_hat[xs[h]] - _dot(w16[xs[h]], st16[h]) for h in hs]
            u16 = [u[h].astype(_bf16) for h in hs]
            emit(1)
            o = [_dot(q_dec[xs[h]], st16[h]) + _dot(attn[xs[h]], u16[h]) for h in hs]
            emit(1)
            st = [st[h] * jnp.exp(glast[xs[h]]) + _dot_tn(k_dec[xs[h]], u16[h]) for h in hs]
            emit(1)
            for h in hs:
                on = o[h] * lax.rsqrt(jnp.mean(o[h] * o[h], axis=-1, keepdims=True) + EPS) * nw
                zg = z_ref[rows, hcol[h]].astype(_f32)
                o_ref[rows, hcol[h]] = (on * _silu(zg)).astype(o_ref.dtype)
        emit(3 * HEADS)
        for h in hs:
            state_ref[h] = st[h]
        return carry

    lax.fori_loop(0, npairs, pair_body, 0)


def _gdn(proj, g, norm_w, B, S, *, lc):
    T = proj.shape[0]
    nsteps = S // lc
    row = lambda b, s: b * nsteps + s
    return pl.pallas_call(
        functools.partial(_gdn_kernel, lc=lc),
        grid=(B, nsteps),
        in_specs=[
            pl.BlockSpec((lc, WIDTH), lambda b, s: (row(b, s), COL_GQ)),
            pl.BlockSpec((lc, WIDTH), lambda b, s: (row(b, s), COL_GK)),
            pl.BlockSpec((lc, WIDTH), lambda b, s: (row(b, s), COL_GV)),
            pl.BlockSpec((lc, WIDTH), lambda b, s: (row(b, s), COL_GZ)),
            pl.BlockSpec((lc, LANES), lambda b, s: (row(b, s), 0)),
            pl.BlockSpec((1, HEAD_DIM), lambda b, s: (0, 0)),
        ],
        out_specs=pl.BlockSpec((lc, WIDTH), lambda b, s: (row(b, s), 0)),
        out_shape=jax.ShapeDtypeStruct((T, WIDTH), _bf16),
        scratch_shapes=[
            pltpu.VMEM((2, GDN_GROUP, 3 * WIDTH), _f32),
            pltpu.VMEM((HEADS, HEAD_DIM, HEAD_DIM), _f32),
        ],
        compiler_params=_params(("arbitrary", "arbitrary")),
        name="gdn",
    )(proj, proj, proj, proj, g, norm_w)


def _merge_kernel(x_ref, yf_ref, yg_ref, gf_ref, gg_ref, wf_ref, wg_ref, wo_ref, o_ref):
    a = _sigmoid(gf_ref[...].astype(_f32)) * _dot(yf_ref[...], wf_ref[...])
    b = _sigmoid(gg_ref[...].astype(_f32)) * _dot(yg_ref[...], wg_ref[...])
    y = (a + b).astype(_bf16)
    o_ref[...] = x_ref[...] + _dot(y, wo_ref[...])


def _merge(x2, y_fox, y_gdn, proj, wf, wg, wo, *, tm):
    T, D = x2.shape
    rows = lambda i: (i, 0)
    const = lambda i: (0, 0)
    return pl.pallas_call(
        _merge_kernel,
        grid=(T // tm,),
        in_specs=[
            pl.BlockSpec((tm, D), rows),
            pl.BlockSpec((tm, WIDTH), rows),
            pl.BlockSpec((tm, WIDTH), rows),
            pl.BlockSpec((tm, D), lambda i: (i, COL_GATE_FOX)),
            pl.BlockSpec((tm, D), lambda i: (i, COL_GATE_GDN)),
            pl.BlockSpec((WIDTH, D), const),
            pl.BlockSpec((WIDTH, D), const),
            pl.BlockSpec((D, D), const),
        ],
        out_specs=pl.BlockSpec((tm, D), rows),
        out_shape=jax.ShapeDtypeStruct((T, D), _f32),
        compiler_params=_params(("arbitrary",)),
        name="merge",
    )(x2, y_fox, y_gdn, proj, proj, wf, wg, wo)


def _ffn_kernel(h_ref, g_ref, wu_ref, cw_ref, wd_ref, gfin_ref, o_ref,
                tail_ref, act_ref, *, tf, final_norm):
    tm = h_ref.shape[0]
    d_ff = wd_ref.shape[0]
    step = pl.program_id(1)

    @pl.when(step == 0)
    def _():
        tail_ref[...] = jnp.zeros(tail_ref.shape, _f32)

    x = h_ref[...]
    hn = (x * lax.rsqrt(jnp.mean(x * x, axis=-1, keepdims=True) + EPS) * g_ref[...]).astype(_bf16)

    for c in range(d_ff // tf):
        c0 = 2 * c * tf
        up = _dot(hn, wu_ref[:, c0:c0 + 2 * tf])
        prev = tail_ref[:, c0:c0 + 2 * tf]
        tail_ref[:, c0:c0 + 2 * tf] = up[tm - SUBLANES:, :]
        y = _causal_conv_rows(up, prev, cw_ref, c0)
        act_ref[:, c * tf:(c + 1) * tf] = (_silu(y[:, :tf]) * y[:, tf:]).astype(_bf16)

    h2 = x + _dot(act_ref[...], wd_ref[...])
    if final_norm:
        h2 = h2 * lax.rsqrt(jnp.mean(h2 * h2, axis=-1, keepdims=True) + EPS) * gfin_ref[...]
    o_ref[...] = h2


def _ffn(h1, g, wu, conv_w, wd, gfin, B, S, *, tm, tf, final_norm):
    T, D = h1.shape
    d_ff = wd.shape[0]
    nsteps = S // tm
    rows = lambda b, s: (b * nsteps + s, 0)
    const = lambda b, s: (0, 0)
    single = pl.Buffered(1)
    return pl.pallas_call(
        functools.partial(_ffn_kernel, tf=tf, final_norm=final_norm),
        grid=(B, nsteps),
        in_specs=[
            pl.BlockSpec((tm, D), rows),
            pl.BlockSpec((1, D), const),
            pl.BlockSpec((D, 2 * d_ff), const, pipeline_mode=single),
            pl.BlockSpec((FFN_CONV, 2 * d_ff), const),
            pl.BlockSpec((d_ff, D), const, pipeline_mode=single),
            pl.BlockSpec((1, D), const),
        ],
        out_specs=pl.BlockSpec((tm, D), rows),
        out_shape=jax.ShapeDtypeStruct((T, D), _f32),
        scratch_shapes=[
            pltpu.VMEM((SUBLANES, 2 * d_ff), _f32),
            pltpu.VMEM((tm, d_ff), _bf16),
        ],
        compiler_params=_params(("arbitrary", "arbitrary")),
        name="ffn",
    )(h1, g, wu, conv_w, wd, gfin)


def _tiles(S):
    pick = lambda pref: next(t for t in pref if S % t == 0)
    return dict(
        tm_in=pick((1024, 512, 256, 128)),
        tq=pick((256, 128)),
        lc=pick((1024, 512, 256, 128)),
        tm_merge=pick((1024, 512, 256, 128)),
        tm_ffn=pick((1024, 512, 256, 128)),
    )


def kernel(x, norm_mix, w_in, fox_f_bias, gdn_conv_w, gdn_a_log, gdn_dt_bias, gdn_norm,
           w_branch_fox, w_branch_gdn, w_out, norm_ffn, w_up, ffn_conv_w, w_down, norm_final):
    B, S, D = x.shape
    L = norm_mix.shape[0]
    T = B * S
    assert D == WIDTH and S % GDN_GROUP == 0
    t = _tiles(S)
    d_ff = w_down.shape[1]
    tf = 256 if d_ff % 256 == 0 else LANES
    h = x.reshape(T, D)
    o = 0
    seg = {}
    for name, width in (("fq", WIDTH), ("fk", WIDTH), ("fv", WIDTH), ("ff", HEADS),
                        ("gq", WIDTH), ("gk", WIDTH), ("gv", WIDTH), ("ga", HEADS), ("gb", HEADS),
                        ("gz", WIDTH), ("gate_fox", D), ("gate_gdn", D)):
        seg[name] = (o, o + width)
        o += width
    for l in range(L):
        w = w_in[l]
        cols = lambda n: w[:, seg[n][0]:seg[n][1]]
        w_big = jnp.concatenate(
            [cols("fq") * (HEAD_DIM ** -0.5 * LOG2E), cols("fk"), cols("fv"), cols("gq"), cols("gk"), cols("gv"),
             cols("gz"), cols("gate_fox"), cols("gate_gdn")], axis=1).astype(_bf16)
        w_small = jnp.concatenate(
            [cols("ff"), cols("ga"), cols("gb"), jnp.zeros((D, LANES - 3 * HEADS), w.dtype)], axis=1).astype(_bf16)
        zpad = jnp.zeros((LANES - 2 * HEADS,), _f32)
        gate_params = jnp.zeros((SUBLANES, LANES), _f32)
        gate_params = gate_params.at[0].set(jnp.concatenate([fox_f_bias[l], gdn_dt_bias[l], zpad]))
        gate_params = gate_params.at[1].set(jnp.concatenate([jnp.zeros((HEADS,), _f32), gdn_a_log[l], zpad]))

        proj, g, negc = _in_proj(h, norm_mix[l][None, :], w_big, w_small, gdn_conv_w[l], gate_params, B, S,
                                 tm=t["tm_in"])
        y_fox = _fox(proj, negc, B, S, tq=t["tq"])
        y_gdn = _gdn(proj, g, gdn_norm[l][None, :], B, S, lc=t["lc"])
        h1 = _merge(h, y_fox, y_gdn, proj, w_branch_fox[l].astype(_bf16), w_branch_gdn[l].astype(_bf16),
                    w_out[l].astype(_bf16), tm=t["tm_merge"])
        pair_chunks = lambda m: m.reshape(m.shape[0], 2, d_ff // tf, tf).transpose(0, 2, 1, 3).reshape(m.shape)
        h = _ffn(h1, norm_ffn[l][None, :], pair_chunks(w_up[l]).astype(_bf16), pair_chunks(ffn_conv_w[l]),
                 w_down[l].astype(_bf16),
                 norm_final[None, :], B, S, tm=t["tm_ffn"], tf=tf, final_norm=(l == L - 1))
    if L == 0:
        raise ValueError("at least one layer expected")
    return h.reshape(B, S, D)
```
